```python
import jax, jax.numpy as jnp
from jax import lax
import numpy as np

D_MODEL = 1024
BATCH = 4
SEQ = 4096
DEPTH = 1

SB_HEADS = 8
SB_HEAD_DIM = 64
SB_WIDTH = SB_HEADS * SB_HEAD_DIM
CONV_CHANNELS = 512
CONV_WIDTH = 3
MEM_TOKENS = 256
MEM_HEADS = 4
MEM_HEAD_DIM = 128
MEM_WIDTH = MEM_HEADS * MEM_HEAD_DIM
N_BRANCHES = 3
N_EXPERTS = 32
TOP_K = 4
D_FF_EXPERT = 1024
SWIGLU_LIMIT = 7.0
SWIGLU_ALPHA = 1.702
Q_BLOCK = 128
MOE_BLOCK = 128
RMS_EPS = 1e-6

IN_PARTS = (SB_WIDTH, SB_WIDTH, SB_WIDTH,
            CONV_CHANNELS, CONV_CHANNELS, CONV_CHANNELS,
            MEM_WIDTH, N_BRANCHES * D_MODEL)
IN_WIDTH = sum(IN_PARTS)
IN_SPLITS = tuple(int(v) for v in np.cumsum(IN_PARTS)[:-1])

kernel_name = "hybrid_stickbreak_shortconv_memattn_moe"


def rms_norm(x, g):
    xf = x.astype(jnp.float32)
    y = xf * lax.rsqrt(jnp.mean(xf * xf, axis=-1, keepdims=True) + RMS_EPS)
    return (y * g.astype(jnp.float32)).astype(x.dtype)


def stick_breaking_attention(q, k, v):
    b, s, h, dh = q.shape
    nb = s // Q_BLOCK
    scale = dh ** -0.5
    kt = k.transpose(0, 2, 1, 3).astype(jnp.float32)
    vt = v.transpose(0, 2, 1, 3).astype(jnp.float32)
    qb = q.reshape(b, nb, Q_BLOCK, h, dh).transpose(1, 0, 3, 2, 4)
    key_pos = jnp.arange(s)

    def block(args):
        q_blk, blk = args
        z = jnp.einsum('bhqd,bhkd->bhqk', q_blk.astype(jnp.float32), kt) * scale
        q_pos = blk * Q_BLOCK + jnp.arange(Q_BLOCK)
        causal = key_pos[None, :] < q_pos[:, None]
        log_stay = jnp.where(causal, jax.nn.log_sigmoid(-z), 0.0)
        log_between = lax.cumsum(log_stay, axis=3, reverse=True) - log_stay
        w = jnp.where(causal, jnp.exp(jax.nn.log_sigmoid(z) + log_between), 0.0)
        return jnp.einsum('bhqk,bhkd->bhqd', w, vt)

    o = lax.map(block, (qb, jnp.arange(nb)))
    return o.transpose(1, 0, 3, 2, 4).reshape(b, s, h * dh).astype(q.dtype)


def short_gated_conv(u_b, u_c, u_x, conv_w):
    c = u_x.shape[-1]
    u = u_c * u_x
    z = lax.conv_general_dilated(
        u, conv_w[:, None, :].astype(u.dtype), window_strides=(1,),
        padding=[(CONV_WIDTH - 1, 0)], dimension_numbers=('NWC', 'WIO', 'NWC'),
        feature_group_count=c)
    return u_b * z


def memory_attention(q, mem_n, w_mem_kv, q_g, k_g):
    b, s, _ = q.shape
    m = mem_n.shape[1]
    kv = mem_n @ w_mem_kv
    k, v = jnp.split(kv, 2, axis=-1)
    q = rms_norm(q.reshape(b, s, MEM_HEADS, MEM_HEAD_DIM), q_g)
    k = rms_norm(k.reshape(b, m, MEM_HEADS, MEM_HEAD_DIM), k_g)
    v = v.reshape(b, m, MEM_HEADS, MEM_HEAD_DIM)
    sc = jnp.einsum('bshd,bmhd->bhsm', q.astype(jnp.float32), k.astype(jnp.float32)) * (MEM_HEAD_DIM ** -0.5)
    p = jax.nn.softmax(sc, axis=-1)
    o = jnp.einsum('bhsm,bmhd->bshd', p, v.astype(jnp.float32))
    return o.reshape(b, s, MEM_WIDTH).astype(q.dtype)


def moe_ffn(h, w_router, b_router, w_gate_up, b_gate_up, w_down, b_down):
    t, d = h.shape
    a = t * TOP_K
    logits = (h @ w_router + b_router).astype(jnp.float32)
    top_val, top_idx = lax.top_k(logits, TOP_K)
    gate = jax.nn.softmax(top_val, axis=-1)
    flat_e = top_idx.reshape(a)
    flat_tok = jnp.arange(a) // TOP_K
    order = jnp.argsort(flat_e, stable=True)
    e_sorted = flat_e[order]
    counts = jnp.bincount(flat_e, length=N_EXPERTS)
    padded = (counts + MOE_BLOCK - 1) // MOE_BLOCK * MOE_BLOCK
    start = jnp.cumsum(counts) - counts
    pend = jnp.cumsum(padded)
    pstart = pend - padded
    dest = pstart[e_sorted] + (jnp.arange(a) - start[e_sorted])
    n_blocks = -(-a // MOE_BLOCK) + N_EXPERTS
    n_rows = n_blocks * MOE_BLOCK
    x_pad = jnp.zeros((n_rows, d), h.dtype).at[dest].set(h[flat_tok[order]])
    block_e = jnp.clip(jnp.searchsorted(pend, jnp.arange(n_blocks) * MOE_BLOCK, side='right'), 0, N_EXPERTS - 1)

    def expert_block(args):
        xb, e = args
        gu = xb @ w_gate_up[e] + b_gate_up[e]
        g, lin = jnp.split(gu, 2, axis=-1)
        g = jnp.minimum(g, SWIGLU_LIMIT)
        lin = jnp.clip(lin, -SWIGLU_LIMIT, SWIGLU_LIMIT)
        act = g * jax.nn.sigmoid(SWIGLU_ALPHA * g) * (lin + 1.0)
        return act @ w_down[e] + b_down[e]

    y_pad = lax.map(expert_block, (x_pad.reshape(n_blocks, MOE_BLOCK, d), block_e)).reshape(n_rows, d)
    y_assign = jnp.zeros((a, d), y_pad.dtype).at[order].set(y_pad[dest]).reshape(t, TOP_K, d)
    return jnp.einsum('tk,tkd->td', gate.astype(y_assign.dtype), y_assign)


def setup_inputs(seed: int = 0) -> dict:
    key = jax.random.key(seed)
    ks = jax.random.split(key, 24)
    f32 = jnp.float32
    nrm = lambda k, shape, s: jax.random.normal(k, shape, f32) * s
    gain = lambda k, shape: 1.0 + 0.02 * jax.random.normal(k, shape, f32)
    L, D, E, F = DEPTH, D_MODEL, N_EXPERTS, D_FF_EXPERT
    return {
        "x": nrm(ks[0], (BATCH, SEQ, D), 1.0),
        "mem": nrm(ks[1], (BATCH, MEM_TOKENS, D), 1.0),
        "mix_norm_g": gain(ks[2], (L, D)),
        "w_in": nrm(ks[3], (L, D, IN_WIDTH), D ** -0.5),
        "sb_q_norm_g": gain(ks[4], (L, SB_HEAD_DIM)),
        "sb_k_norm_g": gain(ks[5], (L, SB_HEAD_DIM)),
        "conv_w": nrm(ks[6], (L, CONV_WIDTH, CONV_CHANNELS), CONV_WIDTH ** -0.5),
        "mem_norm_g": gain(ks[7], (L, D)),
        "w_mem_kv": nrm(ks[8], (L, D, 2 * MEM_WIDTH), D ** -0.5),
        "mem_q_norm_g": gain(ks[9], (L, MEM_HEAD_DIM)),
        "mem_k_norm_g": gain(ks[10], (L, MEM_HEAD_DIM)),
        "w_br_sb": nrm(ks[11], (L, SB_WIDTH, D), SB_WIDTH ** -0.5),
        "w_br_conv": nrm(ks[12], (L, CONV_CHANNELS, D), CONV_CHANNELS ** -0.5),
        "w_br_mem": nrm(ks[13], (L, MEM_WIDTH, D), MEM_WIDTH ** -0.5),
        "w_o": nrm(ks[14], (L, D, D), D ** -0.5),
        "ffn_norm_g": gain(ks[15], (L, D)),
        "w_router": nrm(ks[16], (L, D, E), D ** -0.5),
        "b_router": nrm(ks[17], (L, E), 0.01),
        "w_gate_up": nrm(ks[18], (L, E, D, 2 * F), D ** -0.5),
        "b_gate_up": nrm(ks[19], (L, E, 2 * F), 0.01),
        "w_down": nrm(ks[20], (L, E, F, D), F ** -0.5),
        "b_down": nrm(ks[21], (L, E, D), 0.01),
    }


def reference(x, mem, mix_norm_g, w_in, sb_q_norm_g, sb_k_norm_g, conv_w, mem_norm_g,
              w_mem_kv, mem_q_norm_g, mem_k_norm_g, w_br_sb, w_br_conv, w_br_mem, w_o,
              ffn_norm_g, w_router, b_router, w_gate_up, b_gate_up, w_down, b_down):
    b, s, d = x.shape
    for l in range(DEPTH):
        h = rms_norm(x, mix_norm_g[l])
        proj = h @ w_in[l]
        q_sb, k_sb, v_sb, cb, cc, cx, q_mem, gates = jnp.split(proj, IN_SPLITS, axis=-1)
        q_sb = rms_norm(q_sb.reshape(b, s, SB_HEADS, SB_HEAD_DIM), sb_q_norm_g[l])
        k_sb = rms_norm(k_sb.reshape(b, s, SB_HEADS, SB_HEAD_DIM), sb_k_norm_g[l])
        v_sb = v_sb.reshape(b, s, SB_HEADS, SB_HEAD_DIM)
        o_sb = stick_breaking_attention(q_sb, k_sb, v_sb)
        o_conv = short_gated_conv(cb, cc, cx, conv_w[l])
        mem_n = rms_norm(mem, mem_norm_g[l])
        o_mem = memory_attention(q_mem, mem_n, w_mem_kv[l], mem_q_norm_g[l], mem_k_norm_g[l])
        g = jax.nn.sigmoid(gates.reshape(b, s, N_BRANCHES, d))
        merged = (g[:, :, 0] * (o_sb @ w_br_sb[l])
                  + g[:, :, 1] * (o_conv @ w_br_conv[l])
                  + g[:, :, 2] * (o_mem @ w_br_mem[l]))
        x = x + merged @ w_o[l]
        hf = rms_norm(x, ffn_norm_g[l]).reshape(b * s, d)
        y = moe_ffn(hf, w_router[l], b_router[l], w_gate_up[l], b_gate_up[l], w_down[l], b_down[l])
        x = x + y.reshape(b, s, d).astype(x.dtype)
    return x
```

```python
import functools

import numpy as np
import jax
import jax.numpy as jnp
from jax import lax
from jax.experimental import pallas as pl
from jax.experimental.pallas import tpu as pltpu

F32 = jnp.float32
BF16 = jnp.bfloat16

RMS_EPS = 1e-6
SB_HEADS = 8
SB_HEAD_DIM = 64
MEM_HEADS = 4
MEM_HEAD_DIM = 128
N_EXPERTS = 32
TOP_K = 4
SWIGLU_LIMIT = 7.0
SWIGLU_ALPHA = 1.702

LANES = 128
COL_BLOCK = 512
CB_Q, CB_K, CB_V, CB_CONV_B, CB_CONV_C, CB_CONV_X, CB_MEM_Q, CB_GATES = range(8)

VMEM_LIMIT = 56 * 1024 * 1024


def _dot(a, b):
    return jnp.dot(a, b, preferred_element_type=F32)


def _split_bf16(x):
    hi = x.astype(BF16)
    lo = (x - hi.astype(F32)).astype(BF16)
    return hi, lo


def _group_sum_matrix(width, group):
    idx = np.arange(width) // group
    return jnp.asarray(idx[:, None] == idx[None, :], dtype=BF16)


def _memkv_kernel(mem_ref, g_ref, w_ref, nm_ref, kg_ref, k_ref, v_ref):
    xf = mem_ref[...]
    ms = jnp.mean(xf * xf, axis=-1, keepdims=True)
    h = (xf * lax.rsqrt(ms + RMS_EPS) * g_ref[...]).astype(BF16)
    kv = _dot(h, w_ref[...])
    width = k_ref.shape[-1]
    k = kv[:, :width]
    hi, lo = _split_bf16(k * k)
    ss = _dot(hi, nm_ref[...]) + _dot(lo, nm_ref[...])
    k_ref[...] = (k * lax.rsqrt(ss * (1.0 / MEM_HEAD_DIM) + RMS_EPS) * kg_ref[...]).astype(BF16)
    v_ref[...] = kv[:, width:].astype(BF16)


def _memkv(mem2, g, w_bf, nm128, kg_cols):
    rows, d = mem2.shape
    width = w_bf.shape[1] // 2
    return pl.pallas_call(
        _memkv_kernel,
        out_shape=(jax.ShapeDtypeStruct((rows, width), BF16),
                   jax.ShapeDtypeStruct((rows, width), BF16)),
        compiler_params=pltpu.CompilerParams(vmem_limit_bytes=VMEM_LIMIT),
        name="memkv",
    )(mem2, g, w_bf, nm128, kg_cols)


def _inproj_kernel(x_ref, g_ref, w_ref, nm_ref, cg_ref, o_ref, h_scr):
    j = pl.program_id(1)

    @pl.when(j == 0)
    def _():
        xf = x_ref[...]
        ms = jnp.mean(xf * xf, axis=-1, keepdims=True)
        h_scr[...] = (xf * lax.rsqrt(ms + RMS_EPS) * g_ref[...]).astype(BF16)

    p = _dot(h_scr[...], w_ref[...])
    is_norm = jnp.logical_or(j <= CB_K, j == CB_MEM_Q)

    @pl.when(is_norm)
    def _():
        hi, lo = _split_bf16(p * p)
        nm = nm_ref[0]
        ss = _dot(hi, nm) + _dot(lo, nm)
        inv = jnp.where(j == CB_MEM_Q, 1.0 / MEM_HEAD_DIM, 1.0 / SB_HEAD_DIM)
        o_ref[...] = (p * lax.rsqrt(ss * inv + RMS_EPS) * cg_ref[...]).astype(BF16)

    @pl.when(jnp.logical_not(is_norm))
    def _():
        o_ref[...] = p.astype(BF16)


def _inproj(x2, g, w_bf, nmats, col_gain, tm):
    t, d = x2.shape
    n = w_bf.shape[1]
    grid = (t // tm, n // COL_BLOCK)
    return pl.pallas_call(
        _inproj_kernel,
        grid=grid,
        in_specs=[
            pl.BlockSpec((tm, d), lambda i, j: (i, 0)),
            pl.BlockSpec((1, d), lambda i, j: (0, 0)),
            pl.BlockSpec((d, COL_BLOCK), lambda i, j: (0, j)),
            pl.BlockSpec((1, COL_BLOCK, COL_BLOCK), lambda i, j: (jnp.where(j == CB_MEM_Q, 1, 0), 0, 0)),
            pl.BlockSpec((1, COL_BLOCK), lambda i, j: (0, j)),
        ],
        out_specs=pl.BlockSpec((tm, COL_BLOCK), lambda i, j: (i, j)),
        out_shape=jax.ShapeDtypeStruct((t, n), BF16),
        scratch_shapes=[pltpu.VMEM((tm, d), BF16)],
        compiler_params=pltpu.CompilerParams(
            dimension_semantics=("arbitrary", "arbitrary"), vmem_limit_bytes=VMEM_LIMIT),
        name="inproj",
    )(x2, g, w_bf, nmats, col_gain)


def _sb_kernel(q_ref, kT_ref, v_ref, u_ref, o_ref, qm_scr, carry_scr, acc_scr, *, blk):
    i = pl.program_id(1)
    n_pairs = SB_HEADS // 2
    lane = lax.broadcasted_iota(jnp.int32, (1, LANES), 1)
    head_mask = (lane < SB_HEAD_DIM, lane >= SB_HEAD_DIM)

    for h in range(SB_HEADS):
        p, hh = divmod(h, 2)
        qp = q_ref[:, p * LANES:(p + 1) * LANES]
        qm_scr[h] = jnp.where(head_mask[hh], qp, jnp.zeros_like(qp))
    carry_scr[...] = jnp.zeros_like(carry_scr)
    acc_scr[...] = jnp.zeros_like(acc_scr)

    row = lax.broadcasted_iota(jnp.int32, (blk, blk), 0)
    col = lax.broadcasted_iota(jnp.int32, (blk, blk), 1)

    def key_block(j, diagonal):
        start = pl.multiple_of(j * blk, blk)
        for p in range(n_pairs):
            kTp = kT_ref[0, j, p * LANES:(p + 1) * LANES, :]
            vp = v_ref[pl.ds(start, blk), p * LANES:(p + 1) * LANES]
            for hh in range(2):
                h = 2 * p + hh
                z = _dot(qm_scr[h], kTp)
                log_stay = -(jnp.maximum(z, 0.0) + jnp.log(1.0 + jnp.exp(-jnp.abs(z))))
                log_beta = z + log_stay
                if diagonal:
                    causal = col < row
                    log_stay = jnp.where(causal, log_stay, 0.0)
                hi, lo = _split_bf16(log_stay)
                r = _dot(hi, u_ref[...]) + _dot(lo, u_ref[...])
                between = r[:, :blk] + carry_scr[h]
                w = jnp.exp(log_beta + between)
                if diagonal:
                    w = jnp.where(causal, w, 0.0)
                vm = jnp.where(head_mask[hh], vp, jnp.zeros_like(vp))
                acc_scr[p] += _dot(w.astype(BF16), vm)
                carry_scr[h] += r[:, blk:]

    key_block(i, True)

    def body(t, c):
        key_block(i - 1 - t, False)
        return c

    lax.fori_loop(0, i, body, 0)

    for p in range(n_pairs):
        o_ref[:, p * LANES:(p + 1) * LANES] = acc_scr[p].astype(o_ref.dtype)


def _sb_attention(proj, kT, tri, batch, seq, blk):
    t = proj.shape[0]
    nq = seq // blk
    width = SB_HEADS * SB_HEAD_DIM
    return pl.pallas_call(
        functools.partial(_sb_kernel, blk=blk),
        grid=(batch, nq),
        in_specs=[
            pl.BlockSpec((blk, width), lambda b, i: (b * nq + i, CB_Q)),
            pl.BlockSpec((1, nq, width, blk), lambda b, i: (b, 0, 0, 0)),
            pl.BlockSpec((seq, width), lambda b, i: (b, CB_V)),
            pl.BlockSpec(tri.shape, lambda b, i: (0, 0)),
        ],
        out_specs=pl.BlockSpec((blk, width), lambda b, i: (b * nq + i, 0)),
        out_shape=jax.ShapeDtypeStruct((t, width), BF16),
        scratch_shapes=[
            pltpu.VMEM((SB_HEADS, blk, LANES), BF16),
            pltpu.VMEM((SB_HEADS, blk, LANES), F32),
            pltpu.VMEM((SB_HEADS // 2, blk, LANES), F32),
        ],
        compiler_params=pltpu.CompilerParams(
            dimension_semantics=("arbitrary", "arbitrary"), vmem_limit_bytes=VMEM_LIMIT),
        name="sbattn",
    )(proj, kT, proj, tri)


def _mix_kernel(x_ref, osb_ref, cb_ref, cc_ref, cx_ref, cch_ref, cxh_ref, qm_ref,
                g0a_ref, g0b_ref, g1a_ref, g1b_ref, g2a_ref, g2b_ref,
                kmT_ref, vm_ref, convw_ref, wsb_ref, wcv_ref, wmm_ref, wo_ref,
                fg_ref, wr_ref, br_ref, ltri_ref,
                x1_ref, hf_ref, idx_ref, gate_ref, rank_ref, cnt_ref,
                run_scr, *, tm, tiles_per_seq):
    i = pl.program_id(0)

    @pl.when(i == 0)
    def _():
        run_scr[...] = jnp.zeros_like(run_scr)

    u = cc_ref[...].astype(F32) * cx_ref[...].astype(F32)
    halo = cch_ref[...].astype(F32) * cxh_ref[...].astype(F32)
    halo = jnp.where(i % tiles_per_seq == 0, 0.0, halo)
    prev1 = halo[-1:, :]
    prev2 = halo[-2:-1, :]
    rows = lax.broadcasted_iota(jnp.int32, u.shape, 0)
    u1 = jnp.where(rows == 0, prev1, pltpu.roll(u, 1, 0))
    u2 = jnp.where(rows == 0, prev2, jnp.where(rows == 1, prev1, pltpu.roll(u, 2, 0)))
    cw = convw_ref[...]
    o_conv = cb_ref[...].astype(F32) * (cw[0:1, :] * u2 + cw[1:2, :] * u1 + cw[2:3, :] * u)

    a_mem = None
    for h in range(MEM_HEADS):
        sl = slice(h * MEM_HEAD_DIM, (h + 1) * MEM_HEAD_DIM)
        s = _dot(qm_ref[:, sl], kmT_ref[0, sl, :])
        s = s - jnp.max(s, axis=-1, keepdims=True)
        e = jnp.exp(s)
        pr = e / jnp.sum(e, axis=-1, keepdims=True)
        oh = _dot(pr.astype(BF16), vm_ref[:, sl])
        part = _dot(oh.astype(BF16), wmm_ref[sl, :])
        a_mem = part if a_mem is None else a_mem + part

    a_sb = _dot(osb_ref[...], wsb_ref[...])
    a_cv = _dot(o_conv.astype(BF16), wcv_ref[...])

    half = a_sb.shape[1] // 2
    gates = ((g0a_ref, g1a_ref, g2a_ref), (g0b_ref, g1b_ref, g2b_ref))
    x1 = x_ref[...]
    for c in range(2):
        cs = slice(c * half, (c + 1) * half)
        g0, g1, g2 = (jax.nn.sigmoid(r[...].astype(F32)) for r in gates[c])
        merged = g0 * a_sb[:, cs] + g1 * a_cv[:, cs] + g2 * a_mem[:, cs]
        x1 = x1 + _dot(merged.astype(BF16), wo_ref[cs, :])
    x1_ref[...] = x1

    ms = jnp.mean(x1 * x1, axis=-1, keepdims=True)
    hf = x1 * lax.rsqrt(ms + RMS_EPS) * fg_ref[...]
    hf_ref[...] = hf.astype(BF16)
    logits = jnp.dot(hf, wr_ref[...], preferred_element_type=F32,
                     precision=lax.Precision.HIGHEST) + br_ref[...]
    lane = lax.broadcasted_iota(jnp.int32, logits.shape, 1)
    vals, idxs = [], []
    l = logits
    for _ in range(TOP_K):
        m = jnp.max(l, axis=-1, keepdims=True)
        ik = jnp.min(jnp.where(l == m, lane, LANES), axis=-1, keepdims=True)
        vals.append(m)
        idxs.append(ik)
        l = jnp.where(lane == ik, -3.0e38, l)
    exps = [jnp.exp(v - vals[0]) for v in vals]
    denom = exps[0] + exps[1] + exps[2] + exps[3]

    sel = jnp.zeros(logits.shape, F32)
    for ik in idxs:
        sel = sel + jnp.where(lane == ik, 1.0, 0.0)
    rank_full = _dot(ltri_ref[...], sel.astype(BF16)) + run_scr[...]
    run_scr[...] = run_scr[...] + jnp.sum(sel, axis=0, keepdims=True)
    cnt_ref[...] = jnp.broadcast_to(run_scr[...], cnt_ref.shape)

    idx_out = jnp.zeros(logits.shape, jnp.int32)
    gate_out = jnp.zeros(logits.shape, F32)
    rank_out = jnp.zeros(logits.shape, F32)
    for k in range(TOP_K):
        rk = jnp.sum(jnp.where(lane == idxs[k], rank_full, 0.0), axis=-1, keepdims=True)
        idx_out = jnp.where(lane == k, idxs[k], idx_out)
        gate_out = jnp.where(lane == k, exps[k] / denom, gate_out)
        rank_out = jnp.where(lane == k, rk, rank_out)
    idx_ref[...] = idx_out
    gate_ref[...] = gate_out
    rank_ref[...] = rank_out.astype(jnp.int32)


def _mixer(x2, osb, proj, kmT, vmem, conv_w, wsb, wcv, wmm, wo, fg, wr_pad, br_pad, ltri,
           seq, mem_tokens, tm):
    t, d = x2.shape
    width = COL_BLOCK
    tiles_per_seq = seq // tm
    halo_rows = 16
    hb = tm // halo_rows
    gate_blocks_per_branch = d // COL_BLOCK

    def colspec(cb):
        return pl.BlockSpec((tm, width), lambda i: (i, cb))

    def halospec(cb):
        return pl.BlockSpec((halo_rows, width), lambda i: (jnp.maximum(i * hb - 1, 0), cb))

    def full(a):
        nd = a.ndim
        return pl.BlockSpec(a.shape, lambda i: (0,) * nd)

    gate_specs = [colspec(CB_GATES + br * gate_blocks_per_branch + c)
                  for br in range(3) for c in range(gate_blocks_per_branch)]
    in_specs = [
        pl.BlockSpec((tm, d), lambda i: (i, 0)),
        pl.BlockSpec((tm, width), lambda i: (i, 0)),
        colspec(CB_CONV_B), colspec(CB_CONV_C), colspec(CB_CONV_X),
        halospec(CB_CONV_C), halospec(CB_CONV_X),
        colspec(CB_MEM_Q),
        *gate_specs,
        pl.BlockSpec((1, width, mem_tokens), lambda i: (i // tiles_per_seq, 0, 0)),
        pl.BlockSpec((mem_tokens, width), lambda i: (i // tiles_per_seq, 0)),
        full(conv_w), full(wsb), full(wcv), full(wmm), full(wo), full(fg),
        full(wr_pad), full(br_pad), full(ltri),
    ]
    out_shape = (
        jax.ShapeDtypeStruct((t, d), F32),
        jax.ShapeDtypeStruct((t, d), BF16),
        jax.ShapeDtypeStruct((t, LANES), jnp.int32),
        jax.ShapeDtypeStruct((t, LANES), F32),
        jax.ShapeDtypeStruct((t, LANES), jnp.int32),
        jax.ShapeDtypeStruct((8, LANES), F32),
    )
    out_specs = (
        pl.BlockSpec((tm, d), lambda i: (i, 0)),
        pl.BlockSpec((tm, d), lambda i: (i, 0)),
        pl.BlockSpec((tm, LANES), lambda i: (i, 0)),
        pl.BlockSpec((tm, LANES), lambda i: (i, 0)),
        pl.BlockSpec((tm, LANES), lambda i: (i, 0)),
        pl.BlockSpec((8, LANES), lambda i: (0, 0)),
    )
    n_proj_views = 6 + 3 * gate_blocks_per_branch
    return pl.pallas_call(
        functools.partial(_mix_kernel, tm=tm, tiles_per_seq=tiles_per_seq),
        grid=(t // tm,),
        in_specs=in_specs,
        out_specs=out_specs,
        out_shape=out_shape,
        scratch_shapes=[pltpu.VMEM((1, LANES), F32)],
        compiler_params=pltpu.CompilerParams(
            dimension_semantics=("arbitrary",), vmem_limit_bytes=VMEM_LIMIT),
        name="mixer",
    )(x2, osb, *([proj] * n_proj_views), kmT, vmem, conv_w, wsb, wcv, wmm, wo, fg, wr_pad, br_pad, ltri)


def _expert_kernel(be_ref, nu_ref, x_ref, wgu_ref, bgu_ref, wdn_ref, bdn_ref, y_ref):
    i = pl.program_id(0)

    @pl.when(i < nu_ref[0])
    def _():
        gu = _dot(x_ref[...], wgu_ref[0]) + bgu_ref[0]
        f = gu.shape[1] // 2
        g = jnp.minimum(gu[:, :f], SWIGLU_LIMIT)
        lin = jnp.clip(gu[:, f:], -SWIGLU_LIMIT, SWIGLU_LIMIT)
        act = g * jax.nn.sigmoid(SWIGLU_ALPHA * g) * (lin + 1.0)
        y_ref[...] = (_dot(act.astype(BF16), wdn_ref[0]) + bdn_ref[0]).astype(y_ref.dtype)

    @pl.when(i >= nu_ref[0])
    def _():
        y_ref[...] = jnp.zeros_like(y_ref)


def _experts(block_e, n_used, x_pad, wgu, bgu, wdn, bdn, tmoe):
    n_rows, d = x_pad.shape
    e, _, f2 = wgu.shape
    grid_spec = pltpu.PrefetchScalarGridSpec(
        num_scalar_prefetch=2,
        grid=(n_rows // tmoe,),
        in_specs=[
            pl.BlockSpec((tmoe, d), lambda i, be, nu: (i, 0)),
            pl.BlockSpec((1, d, f2), lambda i, be, nu: (be[i], 0, 0)),
            pl.BlockSpec((1, 1, f2), lambda i, be, nu: (be[i], 0, 0)),
            pl.BlockSpec((1, f2 // 2, d), lambda i, be, nu: (be[i], 0, 0)),
            pl.BlockSpec((1, 1, d), lambda i, be, nu: (be[i], 0, 0)),
        ],
        out_specs=pl.BlockSpec((tmoe, d), lambda i, be, nu: (i, 0)),
    )
    return pl.pallas_call(
        _expert_kernel,
        grid_spec=grid_spec,
        out_shape=jax.ShapeDtypeStruct((n_rows, d), F32),
        compiler_params=pltpu.CompilerParams(
            dimension_semantics=("arbitrary",), vmem_limit_bytes=VMEM_LIMIT),
        name="experts",
    )(block_e, n_used, x_pad, wgu, bgu, wdn, bdn)


def _layer(x, mem, mix_norm_g, w_in, sb_q_norm_g, sb_k_norm_g, conv_w, mem_norm_g, w_mem_kv,
           mem_q_norm_g, mem_k_norm_g, w_br_sb, w_br_conv, w_br_mem, w_o, ffn_norm_g,
           w_router, b_router, w_gate_up, b_gate_up, w_down, b_down):
    b, s, d = x.shape
    m = mem.shape[1]
    t = b * s
    n_in = w_in.shape[1]
    sb_blk = 128
    tm_proj = min(1024, t)
    tm_mix = 256
    tmoe = 256

    x2 = x.reshape(t, d)
    nm64 = _group_sum_matrix(COL_BLOCK, SB_HEAD_DIM)
    nm128 = _group_sum_matrix(COL_BLOCK, MEM_HEAD_DIM)

    col_gain = jnp.ones((n_in,), F32)
    col_gain = col_gain.at[CB_Q * COL_BLOCK:(CB_Q + 1) * COL_BLOCK].set(
        jnp.tile(sb_q_norm_g, SB_HEADS) * (SB_HEAD_DIM ** -0.5))
    col_gain = col_gain.at[CB_K * COL_BLOCK:(CB_K + 1) * COL_BLOCK].set(jnp.tile(sb_k_norm_g, SB_HEADS))
    col_gain = col_gain.at[CB_MEM_Q * COL_BLOCK:(CB_MEM_Q + 1) * COL_BLOCK].set(
        jnp.tile(mem_q_norm_g, MEM_HEADS) * (MEM_HEAD_DIM ** -0.5))

    proj = _inproj(x2, mix_norm_g.reshape(1, d), w_in.astype(BF16),
                   jnp.stack([nm64, nm128]), col_gain.reshape(1, n_in), tm_proj)

    km, vm = _memkv(mem.reshape(b * m, d), mem_norm_g.reshape(1, d), w_mem_kv.astype(BF16), nm128,
                    jnp.tile(mem_k_norm_g, MEM_HEADS).reshape(1, -1))
    kmT = km.reshape(b, m, -1).transpose(0, 2, 1)

    nq = s // sb_blk
    k_sb = proj[:, CB_K * COL_BLOCK:(CB_K + 1) * COL_BLOCK]
    kT = k_sb.reshape(b, nq, sb_blk, COL_BLOCK).transpose(0, 1, 3, 2)
    jj = np.arange(sb_blk)
    tri = np.concatenate([(jj[:, None] > jj[None, :]), np.ones((sb_blk, LANES), bool)], axis=1)
    o_sb = _sb_attention(proj, kT, jnp.asarray(tri, dtype=BF16), b, s, sb_blk)

    wr_pad = jnp.zeros((d, LANES), F32).at[:, :N_EXPERTS].set(w_router)
    br_pad = jnp.full((1, LANES), -1.0e30, F32).at[0, :N_EXPERTS].set(b_router)
    rr = np.arange(tm_mix)
    ltri = jnp.asarray(rr[None, :] < rr[:, None], dtype=BF16)
    x1, hf, idx, gate, rank, cnt = _mixer(
        x2, o_sb, proj, kmT, vm, conv_w, w_br_sb.astype(BF16), w_br_conv.astype(BF16),
        w_br_mem.astype(BF16), w_o.astype(BF16), ffn_norm_g.reshape(1, d), wr_pad, br_pad, ltri,
        s, m, tm_mix)

    a = t * TOP_K
    idx4 = idx[:, :TOP_K]
    gate4 = gate[:, :TOP_K]
    counts = cnt[0, :N_EXPERTS].astype(jnp.int32)
    padded = (counts + tmoe - 1) // tmoe * tmoe
    pend = jnp.cumsum(padded)
    pstart = pend - padded
    dest = pstart[idx4] + rank[:, :TOP_K]
    n_blocks = a // tmoe + N_EXPERTS
    n_rows = n_blocks * tmoe
    block_e = jnp.clip(jnp.searchsorted(pend, jnp.arange(n_blocks) * tmoe, side='right'),
                       0, N_EXPERTS - 1).astype(jnp.int32)
    n_used = (pend[-1] // tmoe).astype(jnp.int32).reshape(1)
    src_tok = jnp.zeros((n_rows,), jnp.int32).at[dest.reshape(a)].set(jnp.arange(a, dtype=jnp.int32) // TOP_K)
    x_pad = hf[src_tok]

    y_pad = _experts(block_e, n_used, x_pad, w_gate_up.astype(BF16), b_gate_up[:, None, :],
                     w_down.astype(BF16), b_down[:, None, :], tmoe)
    y_assign = y_pad[dest.reshape(a)].reshape(t, TOP_K, d)
    out = x1 + jnp.einsum('tk,tkd->td', gate4, y_assign)
    return out.reshape(b, s, d)


def kernel(x, mem, mix_norm_g, w_in, sb_q_norm_g, sb_k_norm_g, conv_w, mem_norm_g, w_mem_kv,
           mem_q_norm_g, mem_k_norm_g, w_br_sb, w_br_conv, w_br_mem, w_o, ffn_norm_g,
           w_router, b_router, w_gate_up, b_gate_up, w_down, b_down):
    depth = mix_norm_g.shape[0]
    for l in range(depth):
        x = _layer(x, mem, mix_norm_g[l], w_in[l], sb_q_norm_g[l], sb_k_norm_g[l], conv_w[l],
                   mem_norm_g[l], w_mem_kv[l], mem_q_norm_g[l], mem_k_norm_g[l], w_br_sb[l],
                   w_br_conv[l], w_br_mem[l], w_o[l], ffn_norm_g[l], w_router[l], b_router[l],
                   w_gate_up[l], b_gate_up[l], w_down[l], b_down[l])
    return x
```

```python
import functools

import numpy as np
import jax
import jax.numpy as jnp
from jax import lax
from jax.experimental import pallas as pl
from jax.experimental.pallas import tpu as pltpu

F32 = jnp.float32
BF16 = jnp.bfloat16

RMS_EPS = 1e-6
SB_HEADS = 8
SB_HEAD_DIM = 64
MEM_HEADS = 4
MEM_HEAD_DIM = 128
N_EXPERTS = 32
TOP_K = 4
SWIGLU_LIMIT = 7.0
SWIGLU_ALPHA = 1.702
LOG2_E = 1.4426950408889634

LANES = 128
COL_BLOCK = 512
CB_Q, CB_K, CB_V, CB_CONV_B, CB_CONV_C, CB_CONV_X, CB_MEM_Q, CB_GATES = range(8)

VMEM_LIMIT = 56 * 1024 * 1024


def _dot(a, b):
    return jnp.dot(a, b, preferred_element_type=F32)


def _split_bf16(x):
    hi = x.astype(BF16)
    lo = (x - hi.astype(F32)).astype(BF16)
    return hi, lo


def _group_sum_matrix(width, group):
    idx = np.arange(width) // group
    return jnp.asarray(idx[:, None] == idx[None, :], dtype=BF16)


def _memkv_kernel(mem_ref, g_ref, w_ref, nm_ref, kg_ref, k_ref, v_ref):
    xf = mem_ref[...]
    ms = jnp.mean(xf * xf, axis=-1, keepdims=True)
    h = (xf * lax.rsqrt(ms + RMS_EPS) * g_ref[...]).astype(BF16)
    kv = _dot(h, w_ref[...])
    width = k_ref.shape[-1]
    k = kv[:, :width]
    hi, lo = _split_bf16(k * k)
    ss = _dot(hi, nm_ref[...]) + _dot(lo, nm_ref[...])
    k_ref[...] = (k * lax.rsqrt(ss * (1.0 / MEM_HEAD_DIM) + RMS_EPS) * kg_ref[...]).astype(BF16)
    v_ref[...] = kv[:, width:].astype(BF16)


def _memkv(mem2, g, w_bf, nm128, kg_cols):
    rows, d = mem2.shape
    width = w_bf.shape[1] // 2
    return pl.pallas_call(
        _memkv_kernel,
        out_shape=(jax.ShapeDtypeStruct((rows, width), BF16),
                   jax.ShapeDtypeStruct((rows, width), BF16)),
        compiler_params=pltpu.CompilerParams(vmem_limit_bytes=VMEM_LIMIT),
        name="memkv",
    )(mem2, g, w_bf, nm128, kg_cols)


def _inproj_kernel(x_ref, g_ref, w_ref, nm_ref, cg_ref, o_ref, h_scr):
    j = pl.program_id(1)

    @pl.when(j == 0)
    def _():
        xf = x_ref[...]
        ms = jnp.mean(xf * xf, axis=-1, keepdims=True)
        h_scr[...] = (xf * lax.rsqrt(ms + RMS_EPS) * g_ref[...]).astype(BF16)

    p = _dot(h_scr[...], w_ref[...])
    is_norm = jnp.logical_or(j <= CB_K, j == CB_MEM_Q)

    @pl.when(is_norm)
    def _():
        hi, lo = _split_bf16(p * p)
        nm = nm_ref[0]
        ss = _dot(hi, nm) + _dot(lo, nm)
        inv = jnp.where(j == CB_MEM_Q, 1.0 / MEM_HEAD_DIM, 1.0 / SB_HEAD_DIM)
        o_ref[...] = (p * lax.rsqrt(ss * inv + RMS_EPS) * cg_ref[...]).astype(BF16)

    @pl.when(jnp.logical_not(is_norm))
    def _():
        o_ref[...] = p.astype(BF16)


def _inproj(x2, g, w_bf, nmats, col_gain, tm):
    t, d = x2.shape
    n = w_bf.shape[1]
    grid = (t // tm, n // COL_BLOCK)
    return pl.pallas_call(
        _inproj_kernel,
        grid=grid,
        in_specs=[
            pl.BlockSpec((tm, d), lambda i, j: (i, 0)),
            pl.BlockSpec((1, d), lambda i, j: (0, 0)),
            pl.BlockSpec((d, COL_BLOCK), lambda i, j: (0, j)),
            pl.BlockSpec((1, COL_BLOCK, COL_BLOCK), lambda i, j: (jnp.where(j == CB_MEM_Q, 1, 0), 0, 0)),
            pl.BlockSpec((1, COL_BLOCK), lambda i, j: (0, j)),
        ],
        out_specs=pl.BlockSpec((tm, COL_BLOCK), lambda i, j: (i, j)),
        out_shape=jax.ShapeDtypeStruct((t, n), BF16),
        scratch_shapes=[pltpu.VMEM((tm, d), BF16)],
        compiler_params=pltpu.CompilerParams(
            dimension_semantics=("arbitrary", "arbitrary"), vmem_limit_bytes=VMEM_LIMIT),
        name="inproj",
    )(x2, g, w_bf, nmats, col_gain)


def _sb_kernel(q_ref, kbd_ref, v_ref, tri_ref, o_ref, carry_scr, acc_scr, *, blk):
    i = pl.program_id(1)
    n_pairs = SB_HEADS // 2
    pair_w = 2 * blk
    tq = q_ref.shape[0]
    kpq = tq // blk
    lane = lax.broadcasted_iota(jnp.int32, (1, LANES), 1)
    row = lax.broadcasted_iota(jnp.int32, (tq, pair_w), 0)
    col = lax.broadcasted_iota(jnp.int32, (tq, pair_w), 1)
    key_in_block = jnp.where(col >= blk, col - blk, col)

    carry_scr[...] = jnp.zeros_like(carry_scr)
    acc_scr[...] = jnp.zeros_like(acc_scr)

    def key_block(j, diagonal):
        start = pl.multiple_of(j * blk, blk)
        pairs = range(n_pairs)
        if diagonal:
            causal = key_in_block + (j - i * kpq) * blk < row
        carries = [carry_scr[p] for p in pairs]
        accs = [acc_scr[p] for p in pairs]
        tri = tri_ref[...]
        zs = [_dot(q_ref[:, p * LANES:(p + 1) * LANES], kbd_ref[0, j, p]) for p in pairs]
        costs, log_betas = [], []
        for p in pairs:
            cost = jnp.maximum(zs[p], 0.0) + jnp.log2(1.0 + jnp.exp2(-jnp.abs(zs[p])))
            log_betas.append(zs[p] - cost)
            costs.append(jnp.where(causal, cost, 0.0) if diagonal else cost)
        rs = [_dot(costs[p].astype(BF16), tri) for p in pairs]
        for p in pairs:
            w = jnp.exp2(log_betas[p] - (rs[p][:, :pair_w] + carries[p]))
            if diagonal:
                w = jnp.where(causal, w, 0.0)
            vp = v_ref[pl.ds(start, blk), p * LANES:(p + 1) * LANES]
            zero = jnp.zeros_like(vp)
            vbd = jnp.concatenate([jnp.where(lane < SB_HEAD_DIM, vp, zero),
                                   jnp.where(lane >= SB_HEAD_DIM, vp, zero)], axis=0)
            accs[p] = accs[p] + _dot(w.astype(BF16), vbd)
            carries[p] = carries[p] + rs[p][:, pair_w:]
        for p in pairs:
            carry_scr[p] = carries[p]
            acc_scr[p] = accs[p]

    for d in reversed(range(kpq)):
        key_block(i * kpq + d, True)

    def body(t, c):
        key_block(i * kpq - 1 - t, False)
        return c

    lax.fori_loop(0, i * kpq, body, 0)

    for p in range(n_pairs):
        o_ref[:, p * LANES:(p + 1) * LANES] = acc_scr[p].astype(o_ref.dtype)


def _sb_attention(proj, kbd, tri, batch, seq, blk, tq):
    t = proj.shape[0]
    nq = seq // tq
    nk = seq // blk
    width = SB_HEADS * SB_HEAD_DIM
    n_pairs = SB_HEADS // 2
    return pl.pallas_call(
        functools.partial(_sb_kernel, blk=blk),
        grid=(batch, nq),
        in_specs=[
            pl.BlockSpec((tq, width), lambda b, i: (b * nq + i, CB_Q)),
            pl.BlockSpec((1, nk, n_pairs, LANES, 2 * blk), lambda b, i: (b, 0, 0, 0, 0)),
            pl.BlockSpec((seq, width), lambda b, i: (b, CB_V)),
            pl.BlockSpec(tri.shape, lambda b, i: (0, 0)),
        ],
        out_specs=pl.BlockSpec((tq, width), lambda b, i: (b * nq + i, 0)),
        out_shape=jax.ShapeDtypeStruct((t, width), BF16),
        scratch_shapes=[
            pltpu.VMEM((n_pairs, tq, 2 * blk), F32),
            pltpu.VMEM((n_pairs, tq, LANES), F32),
        ],
        compiler_params=pltpu.CompilerParams(
            dimension_semantics=("arbitrary", "arbitrary"), vmem_limit_bytes=VMEM_LIMIT),
        name="sbattn",
    )(proj, kbd, proj, tri)


def _mix_kernel(x_ref, osb_ref, cb_ref, cc_ref, cx_ref, cch_ref, cxh_ref, qm_ref,
                g0a_ref, g0b_ref, g1a_ref, g1b_ref, g2a_ref, g2b_ref,
                kmT_ref, vm_ref, convw_ref, wsb_ref, wcv_ref, wmm_ref, wo_ref,
                fg_ref, wr_ref, br_ref, ltri_ref,
                x1_ref, hf_ref, idx_ref, gate_ref, rank_ref, cnt_ref,
                run_scr, *, tm, tiles_per_seq):
    i = pl.program_id(0)

    @pl.when(i == 0)
    def _():
        run_scr[...] = jnp.zeros_like(run_scr)

    u = cc_ref[...].astype(F32) * cx_ref[...].astype(F32)
    halo = cch_ref[...].astype(F32) * cxh_ref[...].astype(F32)
    halo = jnp.where(i % tiles_per_seq == 0, 0.0, halo)
    prev1 = halo[-1:, :]
    prev2 = halo[-2:-1, :]
    rows = lax.broadcasted_iota(jnp.int32, u.shape, 0)
    u1 = jnp.where(rows == 0, prev1, pltpu.roll(u, 1, 0))
    u2 = jnp.where(rows == 0, prev2, jnp.where(rows == 1, prev1, pltpu.roll(u, 2, 0)))
    cw = convw_ref[...]
    o_conv = cb_ref[...].astype(F32) * (cw[0:1, :] * u2 + cw[1:2, :] * u1 + cw[2:3, :] * u)

    a_mem = None
    for h in range(MEM_HEADS):
        sl = slice(h * MEM_HEAD_DIM, (h + 1) * MEM_HEAD_DIM)
        s = _dot(qm_ref[:, sl], kmT_ref[0, sl, :])
        s = s - jnp.max(s, axis=-1, keepdims=True)
        e = jnp.exp(s)
        pr = e / jnp.sum(e, axis=-1, keepdims=True)
        oh = _dot(pr.astype(BF16), vm_ref[:, sl])
        part = _dot(oh.astype(BF16), wmm_ref[sl, :])
        a_mem = part if a_mem is None else a_mem + part

    a_sb = _dot(osb_ref[...], wsb_ref[...])
    a_cv = _dot(o_conv.astype(BF16), wcv_ref[...])

    half = a_sb.shape[1] // 2
    gates = ((g0a_ref, g1a_ref, g2a_ref), (g0b_ref, g1b_ref, g2b_ref))
    x1 = x_ref[...]
    for c in range(2):
        cs = slice(c * half, (c + 1) * half)
        g0, g1, g2 = (jax.nn.sigmoid(r[...].astype(F32)) for r in gates[c])
        merged = g0 * a_sb[:, cs] + g1 * a_cv[:, cs] + g2 * a_mem[:, cs]
        x1 = x1 + _dot(merged.astype(BF16), wo_ref[cs, :])
    x1_ref[...] = x1

    ms = jnp.mean(x1 * x1, axis=-1, keepdims=True)
    hf = x1 * lax.rsqrt(ms + RMS_EPS) * fg_ref[...]
    hf_ref[...] = hf.astype(BF16)
    logits = jnp.dot(hf, wr_ref[...], preferred_element_type=F32,
                     precision=lax.Precision.HIGHEST) + br_ref[...]
    lane = lax.broadcasted_iota(jnp.int32, logits.shape, 1)
    vals, idxs = [], []
    l = logits
    for _ in range(TOP_K):
        m = jnp.max(l, axis=-1, keepdims=True)
        ik = jnp.min(jnp.where(l == m, lane, LANES), axis=-1, keepdims=True)
        vals.append(m)
        idxs.append(ik)
        l = jnp.where(lane == ik, -3.0e38, l)
    exps = [jnp.exp(v - vals[0]) for v in vals]
    denom = exps[0] + exps[1] + exps[2] + exps[3]

    sel = jnp.zeros(logits.shape, F32)
    for ik in idxs:
        sel = sel + jnp.where(lane == ik, 1.0, 0.0)
    rank_full = _dot(ltri_ref[...], sel.astype(BF16)) + run_scr[...]
    run_scr[...] = run_scr[...] + jnp.sum(sel, axis=0, keepdims=True)
    cnt_ref[...] = jnp.broadcast_to(run_scr[...], cnt_ref.shape)

    idx_out = jnp.zeros(logits.shape, jnp.int32)
    gate_out = jnp.zeros(logits.shape, F32)
    rank_out = jnp.zeros(logits.shape, F32)
    for k in range(TOP_K):
        rk = jnp.sum(jnp.where(lane == idxs[k], rank_full, 0.0), axis=-1, keepdims=True)
        idx_out = jnp.where(lane == k, idxs[k], idx_out)
        gate_out = jnp.where(lane == k, exps[k] / denom, gate_out)
        rank_out = jnp.where(lane == k, rk, rank_out)
    idx_ref[...] = idx_out
    gate_ref[...] = gate_out
    rank_ref[...] = rank_out.astype(jnp.int32)


def _mixer(x2, osb, proj, kmT, vmem, conv_w, wsb, wcv, wmm, wo, fg, wr_pad, br_pad, ltri,
           seq, mem_tokens, tm):
    t, d = x2.shape
    width = COL_BLOCK
    tiles_per_seq = seq // tm
    halo_rows = 16
    hb = tm // halo_rows
    gate_blocks_per_branch = d // COL_BLOCK

    def colspec(cb):
        return pl.BlockSpec((tm, width), lambda i: (i, cb))

    def halospec(cb):
        return pl.BlockSpec((halo_rows, width), lambda i: (jnp.maximum(i * hb - 1, 0), cb))

    def full(a):
        nd = a.ndim
        return pl.BlockSpec(a.shape, lambda i: (0,) * nd)

    gate_specs = [colspec(CB_GATES + br * gate_blocks_per_branch + c)
                  for br in range(3) for c in range(gate_blocks_per_branch)]
    in_specs = [
        pl.BlockSpec((tm, d), lambda i: (i, 0)),
        pl.BlockSpec((tm, width), lambda i: (i, 0)),
        colspec(CB_CONV_B), colspec(CB_CONV_C), colspec(CB_CONV_X),
        halospec(CB_CONV_C), halospec(CB_CONV_X),
        colspec(CB_MEM_Q),
        *gate_specs,
        pl.BlockSpec((1, width, mem_tokens), lambda i: (i // tiles_per_seq, 0, 0)),
        pl.BlockSpec((mem_tokens, width), lambda i: (i // tiles_per_seq, 0)),
        full(conv_w), full(wsb), full(wcv), full(wmm), full(wo), full(fg),
        full(wr_pad), full(br_pad), full(ltri),
    ]
    out_shape = (
        jax.ShapeDtypeStruct((t, d), F32),
        jax.ShapeDtypeStruct((t, d), BF16),
        jax.ShapeDtypeStruct((t, LANES), jnp.int32),
        jax.ShapeDtypeStruct((t, LANES), F32),
        jax.ShapeDtypeStruct((t, LANES), jnp.int32),
        jax.ShapeDtypeStruct((8, LANES), F32),
    )
    out_specs = (
        pl.BlockSpec((tm, d), lambda i: (i, 0)),
        pl.BlockSpec((tm, d), lambda i: (i, 0)),
        pl.BlockSpec((tm, LANES), lambda i: (i, 0)),
        pl.BlockSpec((tm, LANES), lambda i: (i, 0)),
        pl.BlockSpec((tm, LANES), lambda i: (i, 0)),
        pl.BlockSpec((8, LANES), lambda i: (0, 0)),
    )
    n_proj_views = 6 + 3 * gate_blocks_per_branch
    return pl.pallas_call(
        functools.partial(_mix_kernel, tm=tm, tiles_per_seq=tiles_per_seq),
        grid=(t // tm,),
        in_specs=in_specs,
        out_specs=out_specs,
        out_shape=out_shape,
        scratch_shapes=[pltpu.VMEM((1, LANES), F32)],
        compiler_params=pltpu.CompilerParams(
            dimension_semantics=("arbitrary",), vmem_limit_bytes=VMEM_LIMIT),
        name="mixer",
    )(x2, osb, *([proj] * n_proj_views), kmT, vmem, conv_w, wsb, wcv, wmm, wo, fg, wr_pad, br_pad, ltri)


def _expert_kernel(be_ref, nu_ref, x_ref, wgu_ref, bgu_ref, wdn_ref, bdn_ref, y_ref):
    i = pl.program_id(0)

    @pl.when(i < nu_ref[0])
    def _():
        gu = _dot(x_ref[...], wgu_ref[0]) + bgu_ref[0]
        f = gu.shape[1] // 2
        g = jnp.minimum(gu[:, :f], SWIGLU_LIMIT)
        lin = jnp.clip(gu[:, f:], -SWIGLU_LIMIT, SWIGLU_LIMIT)
        act = g * jax.nn.sigmoid(SWIGLU_ALPHA * g) * (lin + 1.0)
        y_ref[...] = (_dot(act.astype(BF16), wdn_ref[0]) + bdn_ref[0]).astype(y_ref.dtype)

    @pl.when(i >= nu_ref[0])
    def _():
        y_ref[...] = jnp.zeros_like(y_ref)


def _experts(block_e, n_used, x_pad, wgu, bgu, wdn, bdn, tmoe):
    n_rows, d = x_pad.shape
    e, _, f2 = wgu.shape
    grid_spec = pltpu.PrefetchScalarGridSpec(
        num_scalar_prefetch=2,
        grid=(n_rows // tmoe,),
        in_specs=[
            pl.BlockSpec((tmoe, d), lambda i, be, nu: (i, 0)),
            pl.BlockSpec((1, d, f2), lambda i, be, nu: (be[i], 0, 0)),
            pl.BlockSpec((1, 1, f2), lambda i, be, nu: (be[i], 0, 0)),
            pl.BlockSpec((1, f2 // 2, d), lambda i, be, nu: (be[i], 0, 0)),
            pl.BlockSpec((1, 1, d), lambda i, be, nu: (be[i], 0, 0)),
        ],
        out_specs=pl.BlockSpec((tmoe, d), lambda i, be, nu: (i, 0)),
    )
    return pl.pallas_call(
        _expert_kernel,
        grid_spec=grid_spec,
        out_shape=jax.ShapeDtypeStruct((n_rows, d), F32),
        compiler_params=pltpu.CompilerParams(
            dimension_semantics=("arbitrary",), vmem_limit_bytes=VMEM_LIMIT),
        name="experts",
    )(block_e, n_used, x_pad, wgu, bgu, wdn, bdn)


def _layer(x, mem, mix_norm_g, w_in, sb_q_norm_g, sb_k_norm_g, conv_w, mem_norm_g, w_mem_kv,
           mem_q_norm_g, mem_k_norm_g, w_br_sb, w_br_conv, w_br_mem, w_o, ffn_norm_g,
           w_router, b_router, w_gate_up, b_gate_up, w_down, b_down):
    b, s, d = x.shape
    m = mem.shape[1]
    t = b * s
    n_in = w_in.shape[1]
    sb_blk = 128
    sb_tq = 256
    tm_proj = min(1024, t)
    tm_mix = 256
    tmoe = 256

    x2 = x.reshape(t, d)
    nm64 = _group_sum_matrix(COL_BLOCK, SB_HEAD_DIM)
    nm128 = _group_sum_matrix(COL_BLOCK, MEM_HEAD_DIM)

    col_gain = jnp.ones((n_in,), F32)
    col_gain = col_gain.at[CB_Q * COL_BLOCK:(CB_Q + 1) * COL_BLOCK].set(
        jnp.tile(sb_q_norm_g, SB_HEADS) * (SB_HEAD_DIM ** -0.5 * LOG2_E))
    col_gain = col_gain.at[CB_K * COL_BLOCK:(CB_K + 1) * COL_BLOCK].set(jnp.tile(sb_k_norm_g, SB_HEADS))
    col_gain = col_gain.at[CB_MEM_Q * COL_BLOCK:(CB_MEM_Q + 1) * COL_BLOCK].set(
        jnp.tile(mem_q_norm_g, MEM_HEADS) * (MEM_HEAD_DIM ** -0.5))

    proj = _inproj(x2, mix_norm_g.reshape(1, d), w_in.astype(BF16),
                   jnp.stack([nm64, nm128]), col_gain.reshape(1, n_in), tm_proj)

    km, vm = _memkv(mem.reshape(b * m, d), mem_norm_g.reshape(1, d), w_mem_kv.astype(BF16), nm128,
                    jnp.tile(mem_k_norm_g, MEM_HEADS).reshape(1, -1))
    kmT = km.reshape(b, m, -1).transpose(0, 2, 1)

    nq = s // sb_blk
    k_sb = proj[:, CB_K * COL_BLOCK:(CB_K + 1) * COL_BLOCK]
    kT = k_sb.reshape(b, nq, sb_blk, SB_HEADS // 2, 2, SB_HEAD_DIM).transpose(0, 1, 3, 4, 5, 2)
    eye2 = jnp.eye(2, dtype=BF16)
    kbd = (kT[:, :, :, :, :, None, :] * eye2[None, None, None, :, None, :, None]).reshape(
        b, nq, SB_HEADS // 2, LANES, 2 * sb_blk)
    jj = np.arange(sb_blk)
    later = (jj[:, None] > jj[None, :]).astype(np.float32)
    ones = np.ones((sb_blk, sb_blk), np.float32)
    eye = np.eye(2, dtype=np.float32)
    tri = np.concatenate([np.kron(eye, later), np.kron(eye, ones)], axis=1)
    o_sb = _sb_attention(proj, kbd, jnp.asarray(tri, dtype=BF16), b, s, sb_blk, sb_tq)

    wr_pad = jnp.zeros((d, LANES), F32).at[:, :N_EXPERTS].set(w_router)
    br_pad = jnp.full((1, LANES), -1.0e30, F32).at[0, :N_EXPERTS].set(b_router)
    rr = np.arange(tm_mix)
    ltri = jnp.asarray(rr[None, :] < rr[:, None], dtype=BF16)
    x1, hf, idx, gate, rank, cnt = _mixer(
        x2, o_sb, proj, kmT, vm, conv_w, w_br_sb.astype(BF16), w_br_conv.astype(BF16),
        w_br_mem.astype(BF16), w_o.astype(BF16), ffn_norm_g.reshape(1, d), wr_pad, br_pad, ltri,
        s, m, tm_mix)

    a = t * TOP_K
    idx4 = idx[:, :TOP_K]
    gate4 = gate[:, :TOP_K]
    counts = cnt[0, :N_EXPERTS].astype(jnp.int32)
    padded = (counts + tmoe - 1) // tmoe * tmoe
    pend = jnp.cumsum(padded)
    pstart = pend - padded
    dest = pstart[idx4] + rank[:, :TOP_K]
    n_blocks = a // tmoe + N_EXPERTS
    n_rows = n_blocks * tmoe
    block_row0 = jnp.arange(n_blocks, dtype=jnp.int32) * tmoe
    block_e = jnp.minimum(jnp.sum((pend[None, :] <= block_row0[:, None]).astype(jnp.int32), axis=1),
                          N_EXPERTS - 1)
    n_used = (pend[-1] // tmoe).astype(jnp.int32).reshape(1)
    src_tok = jnp.zeros((n_rows,), jnp.int32).at[dest.reshape(a)].set(jnp.arange(a, dtype=jnp.int32) // TOP_K)
    x_pad = hf[src_tok]

    y_pad = _experts(block_e, n_used, x_pad, w_gate_up.astype(BF16), b_gate_up[:, None, :],
                     w_down.astype(BF16), b_down[:, None, :], tmoe)
    y_assign = y_pad[dest.reshape(a)].reshape(t, TOP_K, d)
    out = x1 + jnp.einsum('tk,tkd->td', gate4, y_assign)
    return out.reshape(b, s, d)


def kernel(x, mem, mix_norm_g, w_in, sb_q_norm_g, sb_k_norm_g, conv_w, mem_norm_g, w_mem_kv,
           mem_q_norm_g, mem_k_norm_g, w_br_sb, w_br_conv, w_br_mem, w_o, ffn_norm_g,
           w_router, b_router, w_gate_up, b_gate_up, w_down, b_down):
    depth = mix_norm_g.shape[0]
    for l in range(depth):
        x = _layer(x, mem, mix_norm_g[l], w_in[l], sb_q_norm_g[l], sb_k_norm_g[l], conv_w[l],
                   mem_norm_g[l], w_mem_kv[l], mem_q_norm_g[l], mem_k_norm_g[l], w_br_sb[l],
                   w_br_conv[l], w_br_mem[l], w_o[l], ffn_norm_g[l], w_router[l], b_router[l],
                   w_gate_up[l], b_gate_up[l], w_down[l], b_down[l])
    return x
```

```python
import functools

import numpy as np
import jax
import jax.numpy as jnp
from jax import lax
from jax.experimental import pallas as pl
from jax.experimental.pallas import tpu as pltpu

F32 = jnp.float32
BF16 = jnp.bfloat16

RMS_EPS = 1e-6
SB_HEADS = 8
SB_HEAD_DIM = 64
MEM_HEADS = 4
MEM_HEAD_DIM = 128
N_EXPERTS = 32
TOP_K = 4
SWIGLU_LIMIT = 7.0
SWIGLU_ALPHA = 1.702
LOG2_E = 1.4426950408889634

LANES = 128
COL_BLOCK = 512
CB_Q, CB_K, CB_V, CB_CONV_B, CB_CONV_C, CB_CONV_X, CB_MEM_Q, CB_GATES = range(8)

VMEM_LIMIT = 56 * 1024 * 1024


def _dot(a, b):
    return jnp.dot(a, b, preferred_element_type=F32)


def _split_bf16(x):
    hi = x.astype(BF16)
    lo = (x - hi.astype(F32)).astype(BF16)
    return hi, lo


def _group_sum_matrix(width, group):
    idx = np.arange(width) // group
    return jnp.asarray(idx[:, None] == idx[None, :], dtype=BF16)


def _memkv_kernel(mem_ref, g_ref, w_ref, nm_ref, kg_ref, k_ref, v_ref):
    xf = mem_ref[...]
    ms = jnp.mean(xf * xf, axis=-1, keepdims=True)
    h = (xf * lax.rsqrt(ms + RMS_EPS) * g_ref[...]).astype(BF16)
    kv = _dot(h, w_ref[...])
    width = k_ref.shape[-1]
    k = kv[:, :width]
    hi, lo = _split_bf16(k * k)
    ss = _dot(hi, nm_ref[...]) + _dot(lo, nm_ref[...])
    k_ref[...] = (k * lax.rsqrt(ss * (1.0 / MEM_HEAD_DIM) + RMS_EPS) * kg_ref[...]).astype(BF16)
    v_ref[...] = kv[:, width:].astype(BF16)


def _memkv(mem2, g, w_bf, nm128, kg_cols):
    rows, d = mem2.shape
    width = w_bf.shape[1] // 2
    return pl.pallas_call(
        _memkv_kernel,
        out_shape=(jax.ShapeDtypeStruct((rows, width), BF16),
                   jax.ShapeDtypeStruct((rows, width), BF16)),
        compiler_params=pltpu.CompilerParams(vmem_limit_bytes=VMEM_LIMIT),
        name="memkv",
    )(mem2, g, w_bf, nm128, kg_cols)


def _inproj_kernel(x_ref, g_ref, w_ref, nm_ref, cg_ref, o_ref, h_scr):
    j = pl.program_id(1)

    @pl.when(j == 0)
    def _():
        xf = x_ref[...]
        ms = jnp.mean(xf * xf, axis=-1, keepdims=True)
        h_scr[...] = (xf * lax.rsqrt(ms + RMS_EPS) * g_ref[...]).astype(BF16)

    p = _dot(h_scr[...], w_ref[...])
    is_norm = jnp.logical_or(j <= CB_K, j == CB_MEM_Q)

    @pl.when(is_norm)
    def _():
        hi, lo = _split_bf16(p * p)
        nm = nm_ref[0]
        ss = _dot(hi, nm) + _dot(lo, nm)
        inv = jnp.where(j == CB_MEM_Q, 1.0 / MEM_HEAD_DIM, 1.0 / SB_HEAD_DIM)
        o_ref[...] = (p * lax.rsqrt(ss * inv + RMS_EPS) * cg_ref[...]).astype(BF16)

    @pl.when(jnp.logical_not(is_norm))
    def _():
        o_ref[...] = p.astype(BF16)


def _inproj(x2, g, w_bf, nmats, col_gain, tm):
    t, d = x2.shape
    n = w_bf.shape[1]
    grid = (t // tm, n // COL_BLOCK)
    return pl.pallas_call(
        _inproj_kernel,
        grid=grid,
        in_specs=[
            pl.BlockSpec((tm, d), lambda i, j: (i, 0)),
            pl.BlockSpec((1, d), lambda i, j: (0, 0)),
            pl.BlockSpec((d, COL_BLOCK), lambda i, j: (0, j)),
            pl.BlockSpec((1, COL_BLOCK, COL_BLOCK), lambda i, j: (jnp.where(j == CB_MEM_Q, 1, 0), 0, 0)),
            pl.BlockSpec((1, COL_BLOCK), lambda i, j: (0, j)),
        ],
        out_specs=pl.BlockSpec((tm, COL_BLOCK), lambda i, j: (i, j)),
        out_shape=jax.ShapeDtypeStruct((t, n), BF16),
        scratch_shapes=[pltpu.VMEM((tm, d), BF16)],
        compiler_params=pltpu.CompilerParams(
            dimension_semantics=("arbitrary", "arbitrary"), vmem_limit_bytes=VMEM_LIMIT),
        name="inproj",
    )(x2, g, w_bf, nmats, col_gain)


def _sb_kernel(q_ref, kbd_ref, v_ref, tri_ref, o_ref, carry_scr, acc_scr, *, blk):
    i = pl.program_id(1)
    n_pairs = SB_HEADS // 2
    pair_w = 2 * blk
    tq = q_ref.shape[0]
    kpq = tq // blk
    lane = lax.broadcasted_iota(jnp.int32, (1, LANES), 1)
    row = lax.broadcasted_iota(jnp.int32, (tq, pair_w), 0)
    col = lax.broadcasted_iota(jnp.int32, (tq, pair_w), 1)
    key_in_block = jnp.where(col >= blk, col - blk, col)

    carry_scr[...] = jnp.zeros_like(carry_scr)
    acc_scr[...] = jnp.zeros_like(acc_scr)

    def key_block(j, diagonal):
        start = pl.multiple_of(j * blk, blk)
        pairs = range(n_pairs)
        if diagonal:
            causal = key_in_block + (j - i * kpq) * blk < row
        carries = [carry_scr[p] for p in pairs]
        accs = [acc_scr[p] for p in pairs]
        tri = tri_ref[...]
        zs = [_dot(q_ref[:, p * LANES:(p + 1) * LANES], kbd_ref[0, j, p]) for p in pairs]
        costs, log_betas = [], []
        for p in pairs:
            cost = jnp.maximum(zs[p], 0.0) + jnp.log2(1.0 + jnp.exp2(-jnp.abs(zs[p])))
            log_betas.append(zs[p] - cost)
            costs.append(jnp.where(causal, cost, 0.0) if diagonal else cost)
        rs = [_dot(costs[p].astype(BF16), tri) for p in pairs]
        for p in pairs:
            w = jnp.exp2(log_betas[p] - (rs[p][:, :pair_w] + carries[p]))
            if diagonal:
                w = jnp.where(causal, w, 0.0)
            vp = v_ref[pl.ds(start, blk), p * LANES:(p + 1) * LANES]
            zero = jnp.zeros_like(vp)
            vbd = jnp.concatenate([jnp.where(lane < SB_HEAD_DIM, vp, zero),
                                   jnp.where(lane >= SB_HEAD_DIM, vp, zero)], axis=0)
            accs[p] = accs[p] + _dot(w.astype(BF16), vbd)
            carries[p] = carries[p] + rs[p][:, pair_w:]
        for p in pairs:
            carry_scr[p] = carries[p]
            acc_scr[p] = accs[p]

    for d in reversed(range(kpq)):
        key_block(i * kpq + d, True)

    def body(t, c):
        key_block(i * kpq - 1 - t, False)
        return c

    lax.fori_loop(0, i * kpq, body, 0)

    for p in range(n_pairs):
        o_ref[:, p * LANES:(p + 1) * LANES] = acc_scr[p].astype(o_ref.dtype)


def _sb_attention(proj, kbd, tri, batch, seq, blk, tq):
    t = proj.shape[0]
    nq = seq // tq
    nk = seq // blk
    width = SB_HEADS * SB_HEAD_DIM
    n_pairs = SB_HEADS // 2
    return pl.pallas_call(
        functools.partial(_sb_kernel, blk=blk),
        grid=(batch, nq),
        in_specs=[
            pl.BlockSpec((tq, width), lambda b, i: (b * nq + i, CB_Q)),
            pl.BlockSpec((1, nk, n_pairs, LANES, 2 * blk), lambda b, i: (b, 0, 0, 0, 0)),
            pl.BlockSpec((seq, width), lambda b, i: (b, CB_V)),
            pl.BlockSpec(tri.shape, lambda b, i: (0, 0)),
        ],
        out_specs=pl.BlockSpec((tq, width), lambda b, i: (b * nq + i, 0)),
        out_shape=jax.ShapeDtypeStruct((t, width), BF16),
        scratch_shapes=[
            pltpu.VMEM((n_pairs, tq, 2 * blk), F32),
            pltpu.VMEM((n_pairs, tq, LANES), F32),
        ],
        compiler_params=pltpu.CompilerParams(
            dimension_semantics=("arbitrary", "arbitrary"), vmem_limit_bytes=VMEM_LIMIT),
        name="sbattn",
    )(proj, kbd, proj, tri)


def _mix_kernel(x_ref, osb_ref, cb_ref, cc_ref, cx_ref, cch_ref, cxh_ref, qm_ref,
                g0a_ref, g0b_ref, g1a_ref, g1b_ref, g2a_ref, g2b_ref,
                kmT_ref, vm_ref, convw_ref, wsb_ref, wcv_ref, wmm_ref, wo_ref,
                fg_ref, wr_ref, br_ref, ltri_ref,
                x1_ref, hf_ref, idx_ref, gate_ref, rank_ref, cnt_ref,
                run_scr, *, tm, tiles_per_seq):
    i = pl.program_id(0)

    @pl.when(i == 0)
    def _():
        run_scr[...] = jnp.zeros_like(run_scr)

    u = cc_ref[...].astype(F32) * cx_ref[...].astype(F32)
    halo = cch_ref[...].astype(F32) * cxh_ref[...].astype(F32)
    halo = jnp.where(i % tiles_per_seq == 0, 0.0, halo)
    prev1 = halo[-1:, :]
    prev2 = halo[-2:-1, :]
    rows = lax.broadcasted_iota(jnp.int32, u.shape, 0)
    u1 = jnp.where(rows == 0, prev1, pltpu.roll(u, 1, 0))
    u2 = jnp.where(rows == 0, prev2, jnp.where(rows == 1, prev1, pltpu.roll(u, 2, 0)))
    cw = convw_ref[...]
    o_conv = cb_ref[...].astype(F32) * (cw[0:1, :] * u2 + cw[1:2, :] * u1 + cw[2:3, :] * u)

    a_mem = None
    for h in range(MEM_HEADS):
        sl = slice(h * MEM_HEAD_DIM, (h + 1) * MEM_HEAD_DIM)
        s = _dot(qm_ref[:, sl], kmT_ref[0, sl, :])
        s = s - jnp.max(s, axis=-1, keepdims=True)
        e = jnp.exp(s)
        pr = e / jnp.sum(e, axis=-1, keepdims=True)
        oh = _dot(pr.astype(BF16), vm_ref[:, sl])
        part = _dot(oh.astype(BF16), wmm_ref[sl, :])
        a_mem = part if a_mem is None else a_mem + part

    a_sb = _dot(osb_ref[...], wsb_ref[...])
    a_cv = _dot(o_conv.astype(BF16), wcv_ref[...])

    half = a_sb.shape[1] // 2
    gates = ((g0a_ref, g1a_ref, g2a_ref), (g0b_ref, g1b_ref, g2b_ref))
    x1 = x_ref[...]
    for c in range(2):
        cs = slice(c * half, (c + 1) * half)
        g0, g1, g2 = (jax.nn.sigmoid(r[...].astype(F32)) for r in gates[c])
        merged = g0 * a_sb[:, cs] + g1 * a_cv[:, cs] + g2 * a_mem[:, cs]
        x1 = x1 + _dot(merged.astype(BF16), wo_ref[cs, :])
    x1_ref[...] = x1

    ms = jnp.mean(x1 * x1, axis=-1, keepdims=True)
    hf = x1 * lax.rsqrt(ms + RMS_EPS) * fg_ref[...]
    hf_ref[...] = hf.reshape(hf_ref.shape)
    logits = jnp.dot(hf, wr_ref[...], preferred_element_type=F32,
                     precision=lax.Precision.HIGHEST) + br_ref[...]
    lane = lax.broadcasted_iota(jnp.int32, logits.shape, 1)
    vals, idxs = [], []
    l = logits
    for _ in range(TOP_K):
        m = jnp.max(l, axis=-1, keepdims=True)
        ik = jnp.min(jnp.where(l == m, lane, LANES), axis=-1, keepdims=True)
        vals.append(m)
        idxs.append(ik)
        l = jnp.where(lane == ik, -3.0e38, l)
    exps = [jnp.exp(v - vals[0]) for v in vals]
    denom = exps[0] + exps[1] + exps[2] + exps[3]

    sel = jnp.zeros(logits.shape, F32)
    for ik in idxs:
        sel = sel + jnp.where(lane == ik, 1.0, 0.0)
    rank_full = _dot(ltri_ref[...], sel.astype(BF16)) + run_scr[...]
    run_scr[...] = run_scr[...] + jnp.sum(sel, axis=0, keepdims=True)
    cnt_ref[...] = jnp.broadcast_to(run_scr[...], cnt_ref.shape)

    idx_out = jnp.zeros(logits.shape, jnp.int32)
    gate_out = jnp.zeros(logits.shape, F32)
    rank_out = jnp.zeros(logits.shape, F32)
    for k in range(TOP_K):
        rk = jnp.sum(jnp.where(lane == idxs[k], rank_full, 0.0), axis=-1, keepdims=True)
        idx_out = jnp.where(lane == k, idxs[k], idx_out)
        gate_out = jnp.where(lane == k, exps[k] / denom, gate_out)
        rank_out = jnp.where(lane == k, rk, rank_out)
    idx_ref[...] = idx_out
    gate_ref[...] = gate_out
    rank_ref[...] = rank_out.astype(jnp.int32)


def _mixer(x2, osb, proj, kmT, vmem, conv_w, wsb, wcv, wmm, wo, fg, wr_pad, br_pad, ltri,
           seq, mem_tokens, tm):
    t, d = x2.shape
    width = COL_BLOCK
    tiles_per_seq = seq // tm
    halo_rows = 16
    hb = tm // halo_rows
    gate_blocks_per_branch = d // COL_BLOCK

    def colspec(cb):
        return pl.BlockSpec((tm, width), lambda i: (i, cb))

    def halospec(cb):
        return pl.BlockSpec((halo_rows, width), lambda i: (jnp.maximum(i * hb - 1, 0), cb))

    def full(a):
        nd = a.ndim
        return pl.BlockSpec(a.shape, lambda i: (0,) * nd)

    gate_specs = [colspec(CB_GATES + br * gate_blocks_per_branch + c)
                  for br in range(3) for c in range(gate_blocks_per_branch)]
    in_specs = [
        pl.BlockSpec((tm, d), lambda i: (i, 0)),
        pl.BlockSpec((tm, width), lambda i: (i, 0)),
        colspec(CB_CONV_B), colspec(CB_CONV_C), colspec(CB_CONV_X),
        halospec(CB_CONV_C), halospec(CB_CONV_X),
        colspec(CB_MEM_Q),
        *gate_specs,
        pl.BlockSpec((1, width, mem_tokens), lambda i: (i // tiles_per_seq, 0, 0)),
        pl.BlockSpec((mem_tokens, width), lambda i: (i // tiles_per_seq, 0)),
        full(conv_w), full(wsb), full(wcv), full(wmm), full(wo), full(fg),
        full(wr_pad), full(br_pad), full(ltri),
    ]
    out_shape = (
        jax.ShapeDtypeStruct((t, d), F32),
        jax.ShapeDtypeStruct((t, 1, d), F32),
        jax.ShapeDtypeStruct((t, LANES), jnp.int32),
        jax.ShapeDtypeStruct((t, LANES), F32),
        jax.ShapeDtypeStruct((t, LANES), jnp.int32),
        jax.ShapeDtypeStruct((8, LANES), F32),
    )
    out_specs = (
        pl.BlockSpec((tm, d), lambda i: (i, 0)),
        pl.BlockSpec((tm, 1, d), lambda i: (i, 0, 0)),
        pl.BlockSpec((tm, LANES), lambda i: (i, 0)),
        pl.BlockSpec((tm, LANES), lambda i: (i, 0)),
        pl.BlockSpec((tm, LANES), lambda i: (i, 0)),
        pl.BlockSpec((8, LANES), lambda i: (0, 0)),
    )
    n_proj_views = 6 + 3 * gate_blocks_per_branch
    return pl.pallas_call(
        functools.partial(_mix_kernel, tm=tm, tiles_per_seq=tiles_per_seq),
        grid=(t // tm,),
        in_specs=in_specs,
        out_specs=out_specs,
        out_shape=out_shape,
        scratch_shapes=[pltpu.VMEM((1, LANES), F32)],
        compiler_params=pltpu.CompilerParams(
            dimension_semantics=("arbitrary",), vmem_limit_bytes=VMEM_LIMIT),
        name="mixer",
    )(x2, osb, *([proj] * n_proj_views), kmT, vmem, conv_w, wsb, wcv, wmm, wo, fg, wr_pad, br_pad, ltri)


def _start_row_gather(idx_ref, first, src_hbm, buf, sem):
    def body(r, c):
        pltpu.make_async_copy(src_hbm.at[idx_ref[first + r]], buf.at[r], sem).start()
        return c

    lax.fori_loop(0, buf.shape[0], body, 0, unroll=8)


def _wait_row_gather(src_hbm, buf, sem):
    pltpu.make_async_copy(src_hbm.at[pl.ds(0, buf.shape[0])], buf, sem).wait()


def _expert_kernel(be_ref, nu_ref, src_ref, hf_hbm, wgu_ref, bgu_ref, wdn_ref, bdn_ref, y_ref,
                   buf0, buf1, x2d, sems):
    i = pl.program_id(0)
    n_used = nu_ref[0]
    tmoe, _, d = buf0.shape
    bufs = (buf0, buf1)

    @pl.when(jnp.logical_and(i == 0, n_used > 0))
    def _():
        _start_row_gather(src_ref, 0, hf_hbm, buf0, sems.at[0])

    for par in range(2):
        @pl.when(jnp.logical_and(i % 2 == par, i < n_used))
        def _():
            @pl.when(i + 1 < n_used)
            def _():
                _start_row_gather(src_ref, (i + 1) * tmoe, hf_hbm, bufs[1 - par], sems.at[1 - par])

            _wait_row_gather(hf_hbm, bufs[par], sems.at[par])
            x2d[...] = bufs[par][...].reshape(tmoe, d)

    @pl.when(i < n_used)
    def _():
        gu = _dot(x2d[...].astype(BF16), wgu_ref[0]) + bgu_ref[0]
        f = gu.shape[1] // 2
        g = jnp.minimum(gu[:, :f], SWIGLU_LIMIT)
        lin = jnp.clip(gu[:, f:], -SWIGLU_LIMIT, SWIGLU_LIMIT)
        act = g * jax.nn.sigmoid(SWIGLU_ALPHA * g) * (lin + 1.0)
        y = _dot(act.astype(BF16), wdn_ref[0]) + bdn_ref[0]
        y_ref[...] = y.reshape(y_ref.shape)

    @pl.when(i >= n_used)
    def _():
        y_ref[...] = jnp.zeros_like(y_ref)


def _experts(block_e, n_used, src_tok, hf3, wgu, bgu, wdn, bdn, tmoe):
    n_rows = src_tok.shape[0]
    d = hf3.shape[-1]
    e, _, f2 = wgu.shape
    grid_spec = pltpu.PrefetchScalarGridSpec(
        num_scalar_prefetch=3,
        grid=(n_rows // tmoe,),
        in_specs=[
            pl.BlockSpec(memory_space=pl.ANY),
            pl.BlockSpec((1, d, f2), lambda i, be, nu, st: (be[i], 0, 0)),
            pl.BlockSpec((1, 1, f2), lambda i, be, nu, st: (be[i], 0, 0)),
            pl.BlockSpec((1, f2 // 2, d), lambda i, be, nu, st: (be[i], 0, 0)),
            pl.BlockSpec((1, 1, d), lambda i, be, nu, st: (be[i], 0, 0)),
        ],
        out_specs=pl.BlockSpec((tmoe, 1, d), lambda i, be, nu, st: (i, 0, 0)),
        scratch_shapes=[
            pltpu.VMEM((tmoe, 1, d), F32),
            pltpu.VMEM((tmoe, 1, d), F32),
            pltpu.VMEM((tmoe, d), F32),
            pltpu.SemaphoreType.DMA((2,)),
        ],
    )
    return pl.pallas_call(
        _expert_kernel,
        grid_spec=grid_spec,
        out_shape=jax.ShapeDtypeStruct((n_rows, 1, d), F32),
        compiler_params=pltpu.CompilerParams(
            dimension_semantics=("arbitrary",), vmem_limit_bytes=VMEM_LIMIT),
        name="experts",
    )(block_e, n_used, src_tok, hf3, wgu, bgu, wdn, bdn)


def _combine_kernel(dest_ref, x1_ref, gate_ref, y_hbm, o_ref, buf0, buf1, y2d, sems):
    i = pl.program_id(0)
    n = pl.num_programs(0)
    rows, _, d = buf0.shape
    tm = rows // TOP_K
    bufs = (buf0, buf1)

    @pl.when(i == 0)
    def _():
        _start_row_gather(dest_ref, 0, y_hbm, buf0, sems.at[0])

    for par in range(2):
        @pl.when(i % 2 == par)
        def _():
            @pl.when(i + 1 < n)
            def _():
                _start_row_gather(dest_ref, (i + 1) * rows, y_hbm, bufs[1 - par], sems.at[1 - par])

            _wait_row_gather(y_hbm, bufs[par], sems.at[par])
            y2d[...] = bufs[par][...].reshape(rows, d)

    acc = x1_ref[...]
    gate = gate_ref[...]
    for k in range(TOP_K):
        acc = acc + gate[:, k:k + 1] * y2d[k * tm:(k + 1) * tm, :]
    o_ref[...] = acc


def _combine(dest_tiled, x1, gate, y3, tm):
    t, d = x1.shape
    rows = tm * TOP_K
    grid_spec = pltpu.PrefetchScalarGridSpec(
        num_scalar_prefetch=1,
        grid=(t // tm,),
        in_specs=[
            pl.BlockSpec((tm, d), lambda i, dr: (i, 0)),
            pl.BlockSpec((tm, LANES), lambda i, dr: (i, 0)),
            pl.BlockSpec(memory_space=pl.ANY),
        ],
        out_specs=pl.BlockSpec((tm, d), lambda i, dr: (i, 0)),
        scratch_shapes=[
            pltpu.VMEM((rows, 1, d), F32),
            pltpu.VMEM((rows, 1, d), F32),
            pltpu.VMEM((rows, d), F32),
            pltpu.SemaphoreType.DMA((2,)),
        ],
    )
    return pl.pallas_call(
        _combine_kernel,
        grid_spec=grid_spec,
        out_shape=jax.ShapeDtypeStruct((t, d), F32),
        compiler_params=pltpu.CompilerParams(
            dimension_semantics=("arbitrary",), vmem_limit_bytes=VMEM_LIMIT),
        name="combine",
    )(dest_tiled, x1, gate, y3)


def _layer(x, mem, mix_norm_g, w_in, sb_q_norm_g, sb_k_norm_g, conv_w, mem_norm_g, w_mem_kv,
           mem_q_norm_g, mem_k_norm_g, w_br_sb, w_br_conv, w_br_mem, w_o, ffn_norm_g,
           w_router, b_router, w_gate_up, b_gate_up, w_down, b_down):
    b, s, d = x.shape
    m = mem.shape[1]
    t = b * s
    n_in = w_in.shape[1]
    sb_blk = 128
    sb_tq = 256
    tm_proj = min(1024, t)
    tm_mix = 256
    tmoe = 256
    tm_comb = 128

    x2 = x.reshape(t, d)
    nm64 = _group_sum_matrix(COL_BLOCK, SB_HEAD_DIM)
    nm128 = _group_sum_matrix(COL_BLOCK, MEM_HEAD_DIM)

    col_gain = jnp.ones((n_in,), F32)
    col_gain = col_gain.at[CB_Q * COL_BLOCK:(CB_Q + 1) * COL_BLOCK].set(
        jnp.tile(sb_q_norm_g, SB_HEADS) * (SB_HEAD_DIM ** -0.5 * LOG2_E))
    col_gain = col_gain.at[CB_K * COL_BLOCK:(CB_K + 1) * COL_BLOCK].set(jnp.tile(sb_k_norm_g, SB_HEADS))
    col_gain = col_gain.at[CB_MEM_Q * COL_BLOCK:(CB_MEM_Q + 1) * COL_BLOCK].set(
        jnp.tile(mem_q_norm_g, MEM_HEADS) * (MEM_HEAD_DIM ** -0.5))

    proj = _inproj(x2, mix_norm_g.reshape(1, d), w_in.astype(BF16),
                   jnp.stack([nm64, nm128]), col_gain.reshape(1, n_in), tm_proj)

    km, vm = _memkv(mem.reshape(b * m, d), mem_norm_g.reshape(1, d), w_mem_kv.astype(BF16), nm128,
                    jnp.tile(mem_k_norm_g, MEM_HEADS).reshape(1, -1))
    kmT = km.reshape(b, m, -1).transpose(0, 2, 1)

    nq = s // sb_blk
    k_sb = proj[:, CB_K * COL_BLOCK:(CB_K + 1) * COL_BLOCK]
    kT = k_sb.reshape(b, nq, sb_blk, SB_HEADS // 2, 2, SB_HEAD_DIM).transpose(0, 1, 3, 4, 5, 2)
    eye2 = jnp.eye(2, dtype=BF16)
    kbd = (kT[:, :, :, :, :, None, :] * eye2[None, None, None, :, None, :, None]).reshape(
        b, nq, SB_HEADS // 2, LANES, 2 * sb_blk)
    jj = np.arange(sb_blk)
    later = (jj[:, None] > jj[None, :]).astype(np.float32)
    ones = np.ones((sb_blk, sb_blk), np.float32)
    eye = np.eye(2, dtype=np.float32)
    tri = np.concatenate([np.kron(eye, later), np.kron(eye, ones)], axis=1)
    o_sb = _sb_attention(proj, kbd, jnp.asarray(tri, dtype=BF16), b, s, sb_blk, sb_tq)

    wr_pad = jnp.zeros((d, LANES), F32).at[:, :N_EXPERTS].set(w_router)
    br_pad = jnp.full((1, LANES), -1.0e30, F32).at[0, :N_EXPERTS].set(b_router)
    rr = np.arange(tm_mix)
    ltri = jnp.asarray(rr[None, :] < rr[:, None], dtype=BF16)
    x1, hf3, idx, gate, rank, cnt = _mixer(
        x2, o_sb, proj, kmT, vm, conv_w, w_br_sb.astype(BF16), w_br_conv.astype(BF16),
        w_br_mem.astype(BF16), w_o.astype(BF16), ffn_norm_g.reshape(1, d), wr_pad, br_pad, ltri,
        s, m, tm_mix)

    a = t * TOP_K
    idx4 = idx[:, :TOP_K]
    counts = cnt[0, :N_EXPERTS].astype(jnp.int32)
    padded = (counts + tmoe - 1) // tmoe * tmoe
    pend = jnp.cumsum(padded)
    pstart = pend - padded
    dest = pstart[idx4] + rank[:, :TOP_K]
    n_blocks = a // tmoe + N_EXPERTS
    n_rows = n_blocks * tmoe
    block_row0 = jnp.arange(n_blocks, dtype=jnp.int32) * tmoe
    block_e = jnp.minimum(jnp.sum((pend[None, :] <= block_row0[:, None]).astype(jnp.int32), axis=1),
                          N_EXPERTS - 1)
    n_used = (pend[-1] // tmoe).astype(jnp.int32).reshape(1)
    src_tok = jnp.zeros((n_rows,), jnp.int32).at[dest.reshape(a)].set(jnp.arange(a, dtype=jnp.int32) // TOP_K)
    y3 = _experts(block_e, n_used, src_tok, hf3, w_gate_up.astype(BF16), b_gate_up[:, None, :],
                  w_down.astype(BF16), b_down[:, None, :], tmoe)
    dest_tiled = dest.reshape(t // tm_comb, tm_comb, TOP_K).transpose(0, 2, 1).reshape(a)
    out = _combine(dest_tiled, x1, gate, y3, tm_comb)
    return out.reshape(b, s, d)


def kernel(x, mem, mix_norm_g, w_in, sb_q_norm_g, sb_k_norm_g, conv_w, mem_norm_g, w_mem_kv,
           mem_q_norm_g, mem_k_norm_g, w_br_sb, w_br_conv, w_br_mem, w_o, ffn_norm_g,
           w_router, b_router, w_gate_up, b_gate_up, w_down, b_down):
    depth = mix_norm_g.shape[0]
    for l in range(depth):
        x = _layer(x, mem, mix_norm_g[l], w_in[l], sb_q_norm_g[l], sb_k_norm_g[l], conv_w[l],
                   mem_norm_g[l], w_mem_kv[l], mem_q_norm_g[l], mem_k_norm_g[l], w_br_sb[l],
                   w_br_conv[l], w_br_mem[l], w_o[l], ffn_norm_g[l], w_router[l], b_router[l],
                   w_gate_up[l], b_gate_up[l], w_down[l], b_down[l])
    return x
```

```python
import functools

import numpy as np
import jax
import jax.numpy as jnp
from jax import lax
from jax.experimental import pallas as pl
from jax.experimental.pallas import tpu as pltpu

F32 = jnp.float32
BF16 = jnp.bfloat16

RMS_EPS = 1e-6
SB_HEADS = 8
SB_HEAD_DIM = 64
MEM_HEADS = 4
MEM_HEAD_DIM = 128
N_EXPERTS = 32
TOP_K = 4
SWIGLU_LIMIT = 7.0
SWIGLU_ALPHA = 1.702
LOG2_E = 1.4426950408889634

LANES = 128
COL_BLOCK = 512
CB_Q, CB_K, CB_V, CB_CONV_B, CB_CONV_C, CB_CONV_X, CB_MEM_Q, CB_GATES = range(8)

VMEM_LIMIT = 56 * 1024 * 1024


def _dot(a, b):
    return jnp.dot(a, b, preferred_element_type=F32)


def _split_bf16(x):
    hi = x.astype(BF16)
    lo = (x - hi.astype(F32)).astype(BF16)
    return hi, lo


def _group_sum_matrix(width, group):
    idx = np.arange(width) // group
    return jnp.asarray(idx[:, None] == idx[None, :], dtype=BF16)


def _memkv_kernel(mem_ref, g_ref, w_ref, nm_ref, kg_ref, k_ref, v_ref):
    xf = mem_ref[...]
    ms = jnp.mean(xf * xf, axis=-1, keepdims=True)
    h = (xf * lax.rsqrt(ms + RMS_EPS) * g_ref[...]).astype(BF16)
    kv = _dot(h, w_ref[...])
    width = k_ref.shape[-1]
    k = kv[:, :width]
    hi, lo = _split_bf16(k * k)
    ss = _dot(hi, nm_ref[...]) + _dot(lo, nm_ref[...])
    k_ref[...] = (k * lax.rsqrt(ss * (1.0 / MEM_HEAD_DIM) + RMS_EPS) * kg_ref[...]).astype(BF16)
    v_ref[...] = kv[:, width:].astype(BF16)


def _memkv(mem2, g, w_bf, nm128, kg_cols):
    rows, d = mem2.shape
    width = w_bf.shape[1] // 2
    return pl.pallas_call(
        _memkv_kernel,
        out_shape=(jax.ShapeDtypeStruct((rows, width), BF16),
                   jax.ShapeDtypeStruct((rows, width), BF16)),
        compiler_params=pltpu.CompilerParams(vmem_limit_bytes=VMEM_LIMIT),
        name="memkv",
    )(mem2, g, w_bf, nm128, kg_cols)


def _inproj_kernel(x_ref, g_ref, w_ref, nm_ref, cg_ref, o_ref, h_scr):
    j = pl.program_id(1)

    @pl.when(j == 0)
    def _():
        xf = x_ref[...]
        ms = jnp.mean(xf * xf, axis=-1, keepdims=True)
        h_scr[...] = (xf * lax.rsqrt(ms + RMS_EPS) * g_ref[...]).astype(BF16)

    p = _dot(h_scr[...], w_ref[...])
    is_norm = jnp.logical_or(j <= CB_K, j == CB_MEM_Q)

    @pl.when(is_norm)
    def _():
        hi, lo = _split_bf16(p * p)
        nm = nm_ref[0]
        ss = _dot(hi, nm) + _dot(lo, nm)
        inv = jnp.where(j == CB_MEM_Q, 1.0 / MEM_HEAD_DIM, 1.0 / SB_HEAD_DIM)
        o_ref[...] = (p * lax.rsqrt(ss * inv + RMS_EPS) * cg_ref[...]).astype(BF16)

    @pl.when(jnp.logical_not(is_norm))
    def _():
        o_ref[...] = p.astype(BF16)


def _inproj(x2, g, w_bf, nmats, col_gain, tm):
    t, d = x2.shape
    n = w_bf.shape[1]
    grid = (t // tm, n // COL_BLOCK)
    return pl.pallas_call(
        _inproj_kernel,
        grid=grid,
        in_specs=[
            pl.BlockSpec((tm, d), lambda i, j: (i, 0)),
            pl.BlockSpec((1, d), lambda i, j: (0, 0)),
            pl.BlockSpec((d, COL_BLOCK), lambda i, j: (0, j)),
            pl.BlockSpec((1, COL_BLOCK, COL_BLOCK), lambda i, j: (jnp.where(j == CB_MEM_Q, 1, 0), 0, 0)),
            pl.BlockSpec((1, COL_BLOCK), lambda i, j: (0, j)),
        ],
        out_specs=pl.BlockSpec((tm, COL_BLOCK), lambda i, j: (i, j)),
        out_shape=jax.ShapeDtypeStruct((t, n), BF16),
        scratch_shapes=[pltpu.VMEM((tm, d), BF16)],
        compiler_params=pltpu.CompilerParams(
            dimension_semantics=("arbitrary", "arbitrary"), vmem_limit_bytes=VMEM_LIMIT),
        name="inproj",
    )(x2, g, w_bf, nmats, col_gain)


def _sb_kernel(q_ref, kbd_ref, v_ref, tri_ref, o_ref, carry_scr, acc_scr, *, blk):
    i = pl.program_id(1)
    n_pairs = SB_HEADS // 2
    pair_w = 2 * blk
    tq = q_ref.shape[0]
    kpq = tq // blk
    lane = lax.broadcasted_iota(jnp.int32, (1, LANES), 1)
    row = lax.broadcasted_iota(jnp.int32, (tq, pair_w), 0)
    col = lax.broadcasted_iota(jnp.int32, (tq, pair_w), 1)
    key_in_block = jnp.where(col >= blk, col - blk, col)

    carry_scr[...] = jnp.zeros_like(carry_scr)
    acc_scr[...] = jnp.zeros_like(acc_scr)

    def key_block(j, diagonal):
        start = pl.multiple_of(j * blk, blk)
        pairs = range(n_pairs)
        if diagonal:
            causal = key_in_block + (j - i * kpq) * blk < row
        carries = [carry_scr[p] for p in pairs]
        accs = [acc_scr[p] for p in pairs]
        tri = tri_ref[...]
        zs = [_dot(q_ref[:, p * LANES:(p + 1) * LANES], kbd_ref[0, j, p]) for p in pairs]
        costs, log_betas = [], []
        for p in pairs:
            cost = jnp.maximum(zs[p], 0.0) + jnp.log2(1.0 + jnp.exp2(-jnp.abs(zs[p])))
            log_betas.append(zs[p] - cost)
            costs.append(jnp.where(causal, cost, 0.0) if diagonal else cost)
        rs = [_dot(costs[p].astype(BF16), tri) for p in pairs]
        for p in pairs:
            w = jnp.exp2(log_betas[p] - (rs[p][:, :pair_w] + carries[p]))
            if diagonal:
                w = jnp.where(causal, w, 0.0)
            vp = v_ref[pl.ds(start, blk), p * LANES:(p + 1) * LANES]
            zero = jnp.zeros_like(vp)
            vbd = jnp.concatenate([jnp.where(lane < SB_HEAD_DIM, vp, zero),
                                   jnp.where(lane >= SB_HEAD_DIM, vp, zero)], axis=0)
            accs[p] = accs[p] + _dot(w.astype(BF16), vbd)
            carries[p] = carries[p] + rs[p][:, pair_w:]
        for p in pairs:
            carry_scr[p] = carries[p]
            acc_scr[p] = accs[p]

    for d in reversed(range(kpq)):
        key_block(i * kpq + d, True)

    def body(t, c):
        key_block(i * kpq - 1 - t, False)
        return c

    lax.fori_loop(0, i * kpq, body, 0)

    for p in range(n_pairs):
        o_ref[:, p * LANES:(p + 1) * LANES] = acc_scr[p].astype(o_ref.dtype)


def _sb_attention(proj, kbd, tri, batch, seq, blk, tq):
    t = proj.shape[0]
    nq = seq // tq
    nk = seq // blk
    width = SB_HEADS * SB_HEAD_DIM
    n_pairs = SB_HEADS // 2
    return pl.pallas_call(
        functools.partial(_sb_kernel, blk=blk),
        grid=(batch, nq),
        in_specs=[
            pl.BlockSpec((tq, width), lambda b, i: (b * nq + i, CB_Q)),
            pl.BlockSpec((1, nk, n_pairs, LANES, 2 * blk), lambda b, i: (b, 0, 0, 0, 0)),
            pl.BlockSpec((seq, width), lambda b, i: (b, CB_V)),
            pl.BlockSpec(tri.shape, lambda b, i: (0, 0)),
        ],
        out_specs=pl.BlockSpec((tq, width), lambda b, i: (b * nq + i, 0)),
        out_shape=jax.ShapeDtypeStruct((t, width), BF16),
        scratch_shapes=[
            pltpu.VMEM((n_pairs, tq, 2 * blk), F32),
            pltpu.VMEM((n_pairs, tq, LANES), F32),
        ],
        compiler_params=pltpu.CompilerParams(
            dimension_semantics=("arbitrary", "arbitrary"), vmem_limit_bytes=VMEM_LIMIT),
        name="sbattn",
    )(proj, kbd, proj, tri)


def _mix_kernel(x_ref, osb_ref, cb_ref, cc_ref, cx_ref, cch_ref, cxh_ref, qm_ref,
                g0a_ref, g0b_ref, g1a_ref, g1b_ref, g2a_ref, g2b_ref,
                kmT_ref, vm_ref, convw_ref, wsb_ref, wcv_ref, wmm_ref, wo_ref,
                fg_ref, wr_ref, br_ref, ltri_ref,
                x1_ref, hf_ref, idx_ref, gate_ref, rank_ref, cnt_ref,
                run_scr, *, tm, tiles_per_seq):
    i = pl.program_id(0)

    @pl.when(i == 0)
    def _():
        run_scr[...] = jnp.zeros_like(run_scr)

    u = cc_ref[...].astype(F32) * cx_ref[...].astype(F32)
    halo = cch_ref[...].astype(F32) * cxh_ref[...].astype(F32)
    halo = jnp.where(i % tiles_per_seq == 0, 0.0, halo)
    prev1 = halo[-1:, :]
    prev2 = halo[-2:-1, :]
    rows = lax.broadcasted_iota(jnp.int32, u.shape, 0)
    u1 = jnp.where(rows == 0, prev1, pltpu.roll(u, 1, 0))
    u2 = jnp.where(rows == 0, prev2, jnp.where(rows == 1, prev1, pltpu.roll(u, 2, 0)))
    cw = convw_ref[...]
    o_conv = cb_ref[...].astype(F32) * (cw[0:1, :] * u2 + cw[1:2, :] * u1 + cw[2:3, :] * u)

    a_mem = None
    for h in range(MEM_HEADS):
        sl = slice(h * MEM_HEAD_DIM, (h + 1) * MEM_HEAD_DIM)
        s = _dot(qm_ref[:, sl], kmT_ref[0, sl, :])
        s = s - jnp.max(s, axis=-1, keepdims=True)
        e = jnp.exp(s)
        pr = e / jnp.sum(e, axis=-1, keepdims=True)
        oh = _dot(pr.astype(BF16), vm_ref[:, sl])
        part = _dot(oh.astype(BF16), wmm_ref[sl, :])
        a_mem = part if a_mem is None else a_mem + part

    a_sb = _dot(osb_ref[...], wsb_ref[...])
    a_cv = _dot(o_conv.astype(BF16), wcv_ref[...])

    half = a_sb.shape[1] // 2
    gates = ((g0a_ref, g1a_ref, g2a_ref), (g0b_ref, g1b_ref, g2b_ref))
    x1 = x_ref[...]
    for c in range(2):
        cs = slice(c * half, (c + 1) * half)
        g0, g1, g2 = (jax.nn.sigmoid(r[...].astype(F32)) for r in gates[c])
        merged = g0 * a_sb[:, cs] + g1 * a_cv[:, cs] + g2 * a_mem[:, cs]
        x1 = x1 + _dot(merged.astype(BF16), wo_ref[cs, :])
    x1_ref[...] = x1

    ms = jnp.mean(x1 * x1, axis=-1, keepdims=True)
    hf = x1 * lax.rsqrt(ms + RMS_EPS) * fg_ref[...]
    hf_ref[...] = hf.reshape(hf_ref.shape)
    logits = jnp.dot(hf, wr_ref[...], preferred_element_type=F32,
                     precision=lax.Precision.HIGHEST) + br_ref[...]
    lane = lax.broadcasted_iota(jnp.int32, logits.shape, 1)
    vals, idxs = [], []
    l = logits
    for _ in range(TOP_K):
        m = jnp.max(l, axis=-1, keepdims=True)
        ik = jnp.min(jnp.where(l == m, lane, LANES), axis=-1, keepdims=True)
        vals.append(m)
        idxs.append(ik)
        l = jnp.where(lane == ik, -3.0e38, l)
    exps = [jnp.exp(v - vals[0]) for v in vals]
    denom = exps[0] + exps[1] + exps[2] + exps[3]

    sel = jnp.zeros(logits.shape, F32)
    for ik in idxs:
        sel = sel + jnp.where(lane == ik, 1.0, 0.0)
    rank_full = _dot(ltri_ref[...], sel.astype(BF16)) + run_scr[...]
    run_scr[...] = run_scr[...] + jnp.sum(sel, axis=0, keepdims=True)
    cnt_ref[...] = jnp.broadcast_to(run_scr[...], cnt_ref.shape)

    idx_out = jnp.zeros(logits.shape, jnp.int32)
    gate_out = jnp.zeros(logits.shape, F32)
    rank_out = jnp.zeros(logits.shape, F32)
    for k in range(TOP_K):
        rk = jnp.sum(jnp.where(lane == idxs[k], rank_full, 0.0), axis=-1, keepdims=True)
        idx_out = jnp.where(lane == k, idxs[k], idx_out)
        gate_out = jnp.where(lane == k, exps[k] / denom, gate_out)
        rank_out = jnp.where(lane == k, rk, rank_out)
    idx_ref[...] = idx_out
    gate_ref[...] = gate_out
    rank_ref[...] = rank_out.astype(jnp.int32)


def _mixer(x2, osb, proj, kmT, vmem, conv_w, wsb, wcv, wmm, wo, fg, wr_pad, br_pad, ltri,
           seq, mem_tokens, tm):
    t, d = x2.shape
    width = COL_BLOCK
    tiles_per_seq = seq // tm
    halo_rows = 16
    hb = tm // halo_rows
    gate_blocks_per_branch = d // COL_BLOCK

    def colspec(cb):
        return pl.BlockSpec((tm, width), lambda i: (i, cb))

    def halospec(cb):
        return pl.BlockSpec((halo_rows, width), lambda i: (jnp.maximum(i * hb - 1, 0), cb))

    def full(a):
        nd = a.ndim
        return pl.BlockSpec(a.shape, lambda i: (0,) * nd)

    gate_specs = [colspec(CB_GATES + br * gate_blocks_per_branch + c)
                  for br in range(3) for c in range(gate_blocks_per_branch)]
    in_specs = [
        pl.BlockSpec((tm, d), lambda i: (i, 0)),
        pl.BlockSpec((tm, width), lambda i: (i, 0)),
        colspec(CB_CONV_B), colspec(CB_CONV_C), colspec(CB_CONV_X),
        halospec(CB_CONV_C), halospec(CB_CONV_X),
        colspec(CB_MEM_Q),
        *gate_specs,
        pl.BlockSpec((1, width, mem_tokens), lambda i: (i // tiles_per_seq, 0, 0)),
        pl.BlockSpec((mem_tokens, width), lambda i: (i // tiles_per_seq, 0)),
        full(conv_w), full(wsb), full(wcv), full(wmm), full(wo), full(fg),
        full(wr_pad), full(br_pad), full(ltri),
    ]
    out_shape = (
        jax.ShapeDtypeStruct((t, d), F32),
        jax.ShapeDtypeStruct((t, 1, d), F32),
        jax.ShapeDtypeStruct((t, LANES), jnp.int32),
        jax.ShapeDtypeStruct((t, LANES), F32),
        jax.ShapeDtypeStruct((t, LANES), jnp.int32),
        jax.ShapeDtypeStruct((8, LANES), F32),
    )
    out_specs = (
        pl.BlockSpec((tm, d), lambda i: (i, 0)),
        pl.BlockSpec((tm, 1, d), lambda i: (i, 0, 0)),
        pl.BlockSpec((tm, LANES), lambda i: (i, 0)),
        pl.BlockSpec((tm, LANES), lambda i: (i, 0)),
        pl.BlockSpec((tm, LANES), lambda i: (i, 0)),
        pl.BlockSpec((8, LANES), lambda i: (0, 0)),
    )
    n_proj_views = 6 + 3 * gate_blocks_per_branch
    return pl.pallas_call(
        functools.partial(_mix_kernel, tm=tm, tiles_per_seq=tiles_per_seq),
        grid=(t // tm,),
        in_specs=in_specs,
        out_specs=out_specs,
        out_shape=out_shape,
        scratch_shapes=[pltpu.VMEM((1, LANES), F32)],
        compiler_params=pltpu.CompilerParams(
            dimension_semantics=("arbitrary",), vmem_limit_bytes=VMEM_LIMIT),
        name="mixer",
    )(x2, osb, *([proj] * n_proj_views), kmT, vmem, conv_w, wsb, wcv, wmm, wo, fg, wr_pad, br_pad, ltri)


DMA_UNROLL = 8


def _start_row_copies(n, make_copy):
    def body(g, c):
        for u in range(DMA_UNROLL):
            make_copy(g * DMA_UNROLL + u).start(priority=u % 2)
        return c

    lax.fori_loop(0, n // DMA_UNROLL, body, 0)


def _dispatch_kernel(dest_ref, pad_ref, hf_ref, xs_hbm, zrow, sems):
    i = pl.program_id(0)
    tm = hf_ref.shape[0]
    n_pad = pad_ref.shape[0]

    @pl.when(i == 0)
    def _():
        zrow[...] = jnp.zeros_like(zrow)
        _start_row_copies(n_pad, lambda j: pltpu.make_async_copy(zrow.at[0], xs_hbm.at[pad_ref[j]], sems.at[1]))

    first = i * (tm * TOP_K)
    _start_row_copies(
        tm * TOP_K,
        lambda q: pltpu.make_async_copy(hf_ref.at[q // TOP_K], xs_hbm.at[dest_ref[first + q]], sems.at[0]))
    for _ in range(TOP_K):
        pltpu.make_async_copy(hf_ref, xs_hbm.at[pl.ds(0, tm)], sems.at[0]).wait()

    @pl.when(i == pl.num_programs(0) - 1)
    def _():
        rows = xs_hbm.at[pl.ds(0, n_pad)]
        pltpu.make_async_copy(rows, rows, sems.at[1]).wait()


def _dispatch(dest_flat, pad_dest, hf3, n_rows_total, tm):
    t, _, d = hf3.shape
    grid_spec = pltpu.PrefetchScalarGridSpec(
        num_scalar_prefetch=2,
        grid=(t // tm,),
        in_specs=[pl.BlockSpec((tm, 1, d), lambda i, dr, pr: (i, 0, 0))],
        out_specs=pl.BlockSpec(memory_space=pl.ANY),
        scratch_shapes=[pltpu.VMEM((8, 1, d), F32), pltpu.SemaphoreType.DMA((2,))],
    )
    return pl.pallas_call(
        _dispatch_kernel,
        grid_spec=grid_spec,
        out_shape=jax.ShapeDtypeStruct((n_rows_total, 1, d), F32),
        compiler_params=pltpu.CompilerParams(
            dimension_semantics=("arbitrary",), vmem_limit_bytes=VMEM_LIMIT),
        name="dispatch",
    )(dest_flat, pad_dest, hf3)


def _expert_kernel(be_ref, nu_ref, x_ref, wgu_ref, bgu_ref, wdn_ref, bdn_ref, y_ref, x2d, wgu_bf, wdn_bf):
    i = pl.program_id(0)
    n_used = nu_ref[0]

    @pl.when(jnp.logical_or(i == 0, be_ref[i] != be_ref[jnp.maximum(i - 1, 0)]))
    def _():
        wgu_bf[...] = wgu_ref[0].astype(BF16)
        wdn_bf[...] = wdn_ref[0].astype(BF16)

    @pl.when(i < n_used)
    def _():
        x2d[...] = x_ref[...].reshape(x2d.shape)
        gu = _dot(x2d[...].astype(BF16), wgu_bf[...]) + bgu_ref[0]
        f = gu.shape[1] // 2
        g = jnp.minimum(gu[:, :f], SWIGLU_LIMIT)
        lin = jnp.clip(gu[:, f:], -SWIGLU_LIMIT, SWIGLU_LIMIT)
        act = g * jax.nn.sigmoid(SWIGLU_ALPHA * g) * (lin + 1.0)
        y = _dot(act.astype(BF16), wdn_bf[...]) + bdn_ref[0]
        y_ref[...] = y.reshape(y_ref.shape)

    @pl.when(i >= n_used)
    def _():
        y_ref[...] = jnp.zeros_like(y_ref)


def _experts(block_e, n_used, xs3, n_rows, wgu, bgu, wdn, bdn, tmoe):
    d = xs3.shape[-1]
    e, _, f2 = wgu.shape

    def x_map(i, be, nu):
        return (jnp.minimum(i, jnp.maximum(nu[0] - 1, 0)), 0, 0)

    grid_spec = pltpu.PrefetchScalarGridSpec(
        num_scalar_prefetch=2,
        grid=(n_rows // tmoe,),
        in_specs=[
            pl.BlockSpec((tmoe, 1, d), x_map),
            pl.BlockSpec((1, d, f2), lambda i, be, nu: (be[i], 0, 0)),
            pl.BlockSpec((1, 1, f2), lambda i, be, nu: (be[i], 0, 0)),
            pl.BlockSpec((1, f2 // 2, d), lambda i, be, nu: (be[i], 0, 0)),
            pl.BlockSpec((1, 1, d), lambda i, be, nu: (be[i], 0, 0)),
        ],
        out_specs=pl.BlockSpec((tmoe, 1, d), lambda i, be, nu: (i, 0, 0)),
        scratch_shapes=[
            pltpu.VMEM((tmoe, d), F32),
            pltpu.VMEM((d, f2), BF16),
            pltpu.VMEM((f2 // 2, d), BF16),
        ],
    )
    return pl.pallas_call(
        _expert_kernel,
        grid_spec=grid_spec,
        out_shape=jax.ShapeDtypeStruct((n_rows, 1, d), F32),
        compiler_params=pltpu.CompilerParams(
            dimension_semantics=("arbitrary",), vmem_limit_bytes=VMEM_LIMIT),
        name="experts",
    )(block_e, n_used, xs3, wgu, bgu, wdn, bdn)


def _combine_kernel(dest_ref, x1_ref, gate_ref, y_hbm, o_ref, buf0, buf1, y2d, sems):
    i = pl.program_id(0)
    n = pl.num_programs(0)
    rows, _, d = buf0.shape
    tm = rows // TOP_K
    bufs = (buf0, buf1)

    def start_gather(tile, slot):
        first = tile * rows
        _start_row_copies(rows, lambda r: pltpu.make_async_copy(
            y_hbm.at[dest_ref[first + r]], bufs[slot].at[r], sems.at[slot]))

    @pl.when(i == 0)
    def _():
        start_gather(0, 0)

    for par in range(2):
        @pl.when(i % 2 == par)
        def _():
            @pl.when(i + 1 < n)
            def _():
                start_gather(i + 1, 1 - par)

            pltpu.make_async_copy(y_hbm.at[pl.ds(0, rows)], bufs[par], sems.at[par]).wait()
            y2d[...] = bufs[par][...].reshape(rows, d)

    acc = x1_ref[...]
    gate = gate_ref[...]
    for k in range(TOP_K):
        acc = acc + gate[:, k:k + 1] * y2d[k * tm:(k + 1) * tm, :]
    o_ref[...] = acc


def _combine(dest_tiled, x1, gate, y3, tm):
    t, d = x1.shape
    rows = tm * TOP_K
    grid_spec = pltpu.PrefetchScalarGridSpec(
        num_scalar_prefetch=1,
        grid=(t // tm,),
        in_specs=[
            pl.BlockSpec((tm, d), lambda i, dr: (i, 0)),
            pl.BlockSpec((tm, LANES), lambda i, dr: (i, 0)),
            pl.BlockSpec(memory_space=pl.ANY),
        ],
        out_specs=pl.BlockSpec((tm, d), lambda i, dr: (i, 0)),
        scratch_shapes=[
            pltpu.VMEM((rows, 1, d), F32),
            pltpu.VMEM((rows, 1, d), F32),
            pltpu.VMEM((rows, d), F32),
            pltpu.SemaphoreType.DMA((2,)),
        ],
    )
    return pl.pallas_call(
        _combine_kernel,
        grid_spec=grid_spec,
        out_shape=jax.ShapeDtypeStruct((t, d), F32),
        compiler_params=pltpu.CompilerParams(
            dimension_semantics=("arbitrary",), vmem_limit_bytes=VMEM_LIMIT),
        name="combine",
    )(dest_tiled, x1, gate, y3)


def _layer(x, mem, mix_norm_g, w_in, sb_q_norm_g, sb_k_norm_g, conv_w, mem_norm_g, w_mem_kv,
           mem_q_norm_g, mem_k_norm_g, w_br_sb, w_br_conv, w_br_mem, w_o, ffn_norm_g,
           w_router, b_router, w_gate_up, b_gate_up, w_down, b_down):
    b, s, d = x.shape
    m = mem.shape[1]
    t = b * s
    n_in = w_in.shape[1]
    sb_blk = 128
    sb_tq = 256
    tm_proj = min(1024, t)
    tm_mix = 256
    tmoe = 256
    tm_comb = 128
    tm_disp = 512

    x2 = x.reshape(t, d)
    nm64 = _group_sum_matrix(COL_BLOCK, SB_HEAD_DIM)
    nm128 = _group_sum_matrix(COL_BLOCK, MEM_HEAD_DIM)

    col_gain = jnp.ones((n_in,), F32)
    col_gain = col_gain.at[CB_Q * COL_BLOCK:(CB_Q + 1) * COL_BLOCK].set(
        jnp.tile(sb_q_norm_g, SB_HEADS) * (SB_HEAD_DIM ** -0.5 * LOG2_E))
    col_gain = col_gain.at[CB_K * COL_BLOCK:(CB_K + 1) * COL_BLOCK].set(jnp.tile(sb_k_norm_g, SB_HEADS))
    col_gain = col_gain.at[CB_MEM_Q * COL_BLOCK:(CB_MEM_Q + 1) * COL_BLOCK].set(
        jnp.tile(mem_q_norm_g, MEM_HEADS) * (MEM_HEAD_DIM ** -0.5))

    proj = _inproj(x2, mix_norm_g.reshape(1, d), w_in.astype(BF16),
                   jnp.stack([nm64, nm128]), col_gain.reshape(1, n_in), tm_proj)

    km, vm = _memkv(mem.reshape(b * m, d), mem_norm_g.reshape(1, d), w_mem_kv.astype(BF16), nm128,
                    jnp.tile(mem_k_norm_g, MEM_HEADS).reshape(1, -1))
    kmT = km.reshape(b, m, -1).transpose(0, 2, 1)

    nq = s // sb_blk
    k_sb = proj[:, CB_K * COL_BLOCK:(CB_K + 1) * COL_BLOCK]
    kT = k_sb.reshape(b, nq, sb_blk, SB_HEADS // 2, 2, SB_HEAD_DIM).transpose(0, 1, 3, 4, 5, 2)
    eye2 = jnp.eye(2, dtype=BF16)
    kbd = (kT[:, :, :, :, :, None, :] * eye2[None, None, None, :, None, :, None]).reshape(
        b, nq, SB_HEADS // 2, LANES, 2 * sb_blk)
    jj = np.arange(sb_blk)
    later = (jj[:, None] > jj[None, :]).astype(np.float32)
    ones = np.ones((sb_blk, sb_blk), np.float32)
    eye = np.eye(2, dtype=np.float32)
    tri = np.concatenate([np.kron(eye, later), np.kron(eye, ones)], axis=1)
    o_sb = _sb_attention(proj, kbd, jnp.asarray(tri, dtype=BF16), b, s, sb_blk, sb_tq)

    wr_pad = jnp.zeros((d, LANES), F32).at[:, :N_EXPERTS].set(w_router)
    br_pad = jnp.full((1, LANES), -1.0e30, F32).at[0, :N_EXPERTS].set(b_router)
    rr = np.arange(tm_mix)
    ltri = jnp.asarray(rr[None, :] < rr[:, None], dtype=BF16)
    x1, hf3, idx, gate, rank, cnt = _mixer(
        x2, o_sb, proj, kmT, vm, conv_w, w_br_sb.astype(BF16), w_br_conv.astype(BF16),
        w_br_mem.astype(BF16), w_o.astype(BF16), ffn_norm_g.reshape(1, d), wr_pad, br_pad, ltri,
        s, m, tm_mix)

    a = t * TOP_K
    idx4 = idx[:, :TOP_K]
    counts = cnt[0, :N_EXPERTS].astype(jnp.int32)
    padded = (counts + tmoe - 1) // tmoe * tmoe
    pend = jnp.cumsum(padded)
    pstart = pend - padded
    dest = pstart[idx4] + rank[:, :TOP_K]
    n_blocks = a // tmoe + N_EXPERTS
    n_rows = n_blocks * tmoe
    block_row0 = jnp.arange(n_blocks, dtype=jnp.int32) * tmoe
    block_e = jnp.minimum(jnp.sum((pend[None, :] <= block_row0[:, None]).astype(jnp.int32), axis=1),
                          N_EXPERTS - 1)
    n_used = (pend[-1] // tmoe).astype(jnp.int32).reshape(1)
    slot = jnp.arange(tmoe, dtype=jnp.int32)
    is_pad = slot[None, :] < (padded - counts)[:, None]
    spare = jnp.logical_not(is_pad).reshape(-1).astype(jnp.int32)
    tail_row = pend[-1] + jnp.cumsum(spare) - spare
    pad_dest = jnp.where(is_pad.reshape(-1), ((pstart + counts)[:, None] + slot[None, :]).reshape(-1), tail_row)
    xs3 = _dispatch(dest.reshape(a), pad_dest, hf3, n_rows, tm_disp)
    y3 = _experts(block_e, n_used, xs3, n_rows, w_gate_up, b_gate_up[:, None, :],
                  w_down, b_down[:, None, :], tmoe)
    dest_tiled = dest.reshape(t // tm_comb, tm_comb, TOP_K).transpose(0, 2, 1).reshape(a)
    out = _combine(dest_tiled, x1, gate, y3, tm_comb)
    return out.reshape(b, s, d)


def kernel(x, mem, mix_norm_g, w_in, sb_q_norm_g, sb_k_norm_g, conv_w, mem_norm_g, w_mem_kv,
           mem_q_norm_g, mem_k_norm_g, w_br_sb, w_br_conv, w_br_mem, w_o, ffn_norm_g,
           w_router, b_router, w_gate_up, b_gate_up, w_down, b_down):
    depth = mix_norm_g.shape[0]
    for l in range(depth):
        x = _layer(x, mem, mix_norm_g[l], w_in[l], sb_q_norm_g[l], sb_k_norm_g[l], conv_w[l],
                   mem_norm_g[l], w_mem_kv[l], mem_q_norm_g[l], mem_k_norm_g[l], w_br_sb[l],
                   w_br_conv[l], w_br_mem[l], w_o[l], ffn_norm_g[l], w_router[l], b_router[l],
                   w_gate_up[l], b_gate_up[l], w_down[l], b_down[l])
    return x
```

```python
import functools

import numpy as np
import jax
import jax.numpy as jnp
from jax import lax
from jax.experimental import pallas as pl
from jax.experimental.pallas import tpu as pltpu

F32 = jnp.float32
BF16 = jnp.bfloat16

RMS_EPS = 1e-6
SB_HEADS = 8
SB_HEAD_DIM = 64
MEM_HEADS = 4
MEM_HEAD_DIM = 128
N_EXPERTS = 32
TOP_K = 4
SWIGLU_LIMIT = 7.0
SWIGLU_ALPHA = 1.702
LOG2_E = 1.4426950408889634

LANES = 128
COL_BLOCK = 512
CB_Q, CB_K, CB_V, CB_CONV_B, CB_CONV_C, CB_CONV_X, CB_MEM_Q, CB_GATES = range(8)

VMEM_LIMIT = 56 * 1024 * 1024


def _dot(a, b):
    return jnp.dot(a, b, preferred_element_type=F32)


def _split_bf16(x):
    hi = x.astype(BF16)
    lo = (x - hi.astype(F32)).astype(BF16)
    return hi, lo


def _group_sum_matrix(width, group):
    idx = np.arange(width) // group
    return jnp.asarray(idx[:, None] == idx[None, :], dtype=BF16)


def _memkv_kernel(mem_ref, g_ref, w_ref, nm_ref, kg_ref, k_ref, v_ref):
    xf = mem_ref[...]
    ms = jnp.mean(xf * xf, axis=-1, keepdims=True)
    h = (xf * lax.rsqrt(ms + RMS_EPS) * g_ref[...]).astype(BF16)
    kv = _dot(h, w_ref[...])
    width = k_ref.shape[-1]
    k = kv[:, :width]
    hi, lo = _split_bf16(k * k)
    ss = _dot(hi, nm_ref[...]) + _dot(lo, nm_ref[...])
    k_ref[...] = (k * lax.rsqrt(ss * (1.0 / MEM_HEAD_DIM) + RMS_EPS) * kg_ref[...]).astype(BF16)
    v_ref[...] = kv[:, width:].astype(BF16)


def _memkv(mem2, g, w_bf, nm128, kg_cols):
    rows, d = mem2.shape
    width = w_bf.shape[1] // 2
    return pl.pallas_call(
        _memkv_kernel,
        out_shape=(jax.ShapeDtypeStruct((rows, width), BF16),
                   jax.ShapeDtypeStruct((rows, width), BF16)),
        compiler_params=pltpu.CompilerParams(vmem_limit_bytes=VMEM_LIMIT),
        name="memkv",
    )(mem2, g, w_bf, nm128, kg_cols)


def _inproj_kernel(x_ref, g_ref, w_ref, nm_ref, cg_ref, o_ref, kT_ref):
    xf = x_ref[...]
    ms = jnp.mean(xf * xf, axis=-1, keepdims=True)
    h = (xf * lax.rsqrt(ms + RMS_EPS) * g_ref[...]).astype(BF16)
    for j in range(w_ref.shape[1] // COL_BLOCK):
        cs = slice(j * COL_BLOCK, (j + 1) * COL_BLOCK)
        p = _dot(h, w_ref[:, cs])
        if j in (CB_Q, CB_K, CB_MEM_Q):
            group, nm = (MEM_HEAD_DIM, nm_ref[1]) if j == CB_MEM_Q else (SB_HEAD_DIM, nm_ref[0])
            ss = _dot((p * p).astype(BF16), nm)
            p = p * lax.rsqrt(ss * (1.0 / group) + RMS_EPS) * cg_ref[:, cs]
        if j == CB_K:
            for c in range(kT_ref.shape[0]):
                kT_ref[c] = p[c * LANES:(c + 1) * LANES, :].T.astype(BF16)
        o_ref[:, cs] = p.astype(BF16)


def _inproj(x2, g, w_bf, nmats, col_gain, tm):
    t, d = x2.shape
    n = w_bf.shape[1]

    def resident(a):
        nd = a.ndim
        return pl.BlockSpec(a.shape, lambda i: (0,) * nd)

    return pl.pallas_call(
        _inproj_kernel,
        grid=(t // tm,),
        in_specs=[pl.BlockSpec((tm, d), lambda i: (i, 0)),
                  resident(g), resident(w_bf), resident(nmats), resident(col_gain)],
        out_specs=(pl.BlockSpec((tm, n), lambda i: (i, 0)),
                   pl.BlockSpec((tm // LANES, COL_BLOCK, LANES), lambda i: (i, 0, 0))),
        out_shape=(jax.ShapeDtypeStruct((t, n), BF16),
                   jax.ShapeDtypeStruct((t // LANES, COL_BLOCK, LANES), BF16)),
        compiler_params=pltpu.CompilerParams(
            dimension_semantics=("arbitrary",), vmem_limit_bytes=VMEM_LIMIT),
        name="inproj",
    )(x2, g, w_bf, nmats, col_gain)


def _sb_kernel(q_ref, kT_ref, v_ref, tri_ref, o_ref, carry_scr, acc_scr, *, blk):
    i = pl.program_id(1)
    n_pairs = SB_HEADS // 2
    pair_w = 2 * blk
    tq = q_ref.shape[0]
    kpq = tq // blk
    lane = lax.broadcasted_iota(jnp.int32, (1, LANES), 1)
    row = lax.broadcasted_iota(jnp.int32, (tq, pair_w), 0)
    col = lax.broadcasted_iota(jnp.int32, (tq, pair_w), 1)
    key_in_block = jnp.where(col >= blk, col - blk, col)
    head_dim = lax.broadcasted_iota(jnp.int32, (LANES, blk), 0)

    pairs = range(n_pairs)

    def key_block(j, carries, accs, diagonal):
        start = pl.multiple_of(j * blk, blk)
        carries, accs = list(carries), list(accs)
        if diagonal:
            causal = key_in_block + (j - i * kpq) * blk < row
        tri = tri_ref[...]
        zs = []
        for p in pairs:
            kTp = kT_ref[j, p * LANES:(p + 1) * LANES, :]
            zero = jnp.zeros_like(kTp)
            kbd = jnp.concatenate([jnp.where(head_dim < SB_HEAD_DIM, kTp, zero),
                                   jnp.where(head_dim >= SB_HEAD_DIM, kTp, zero)], axis=1)
            zs.append(_dot(q_ref[:, p * LANES:(p + 1) * LANES], kbd))
        costs = []
        for p in pairs:
            cost = jnp.maximum(zs[p], 0.0) + jnp.log2(1.0 + jnp.exp2(-jnp.abs(zs[p])))
            costs.append(jnp.where(causal, cost, 0.0) if diagonal else cost)
        rs = [_dot(costs[p].astype(BF16), tri) for p in pairs]
        for p in pairs:
            w = jnp.exp2(zs[p] - (rs[p][:, :pair_w] + carries[p]))
            if diagonal:
                w = jnp.where(causal, w, 0.0)
            vp = v_ref[pl.ds(start, blk), p * LANES:(p + 1) * LANES]
            zero = jnp.zeros_like(vp)
            vbd = jnp.concatenate([jnp.where(lane < SB_HEAD_DIM, vp, zero),
                                   jnp.where(lane >= SB_HEAD_DIM, vp, zero)], axis=0)
            accs[p] = accs[p] + _dot(w.astype(BF16), vbd)
            carries[p] = carries[p] + rs[p][:, pair_w:]
        return carries, accs

    def save(carries, accs):
        for p in pairs:
            carry_scr[p] = carries[p]
            acc_scr[p] = accs[p]

    carries = [jnp.zeros((tq, pair_w), F32)] * n_pairs
    accs = [jnp.zeros((tq, LANES), F32)] * n_pairs
    for d in reversed(range(kpq)):
        carries, accs = key_block(i * kpq + d, carries, accs, True)
    save(carries, accs)

    def body(t, c):
        carries = [carry_scr[p] for p in pairs]
        accs = [acc_scr[p] for p in pairs]
        for u in range(kpq):
            carries, accs = key_block((i - t) * kpq - 1 - u, carries, accs, False)
        save(carries, accs)
        return c

    lax.fori_loop(0, i, body, 0)

    for p in pairs:
        o_ref[:, p * LANES:(p + 1) * LANES] = acc_scr[p].astype(o_ref.dtype)


def _sb_attention(proj, kT, tri, batch, seq, blk, tq):
    t = proj.shape[0]
    nq = seq // tq
    nk = seq // blk
    width = SB_HEADS * SB_HEAD_DIM
    n_pairs = SB_HEADS // 2
    return pl.pallas_call(
        functools.partial(_sb_kernel, blk=blk),
        grid=(batch, nq),
        in_specs=[
            pl.BlockSpec((tq, width), lambda b, i: (b * nq + i, CB_Q)),
            pl.BlockSpec((nk, width, blk), lambda b, i: (b, 0, 0)),
            pl.BlockSpec((seq, width), lambda b, i: (b, CB_V)),
            pl.BlockSpec(tri.shape, lambda b, i: (0, 0)),
        ],
        out_specs=pl.BlockSpec((tq, width), lambda b, i: (b * nq + i, 0)),
        out_shape=jax.ShapeDtypeStruct((t, width), BF16),
        scratch_shapes=[
            pltpu.VMEM((n_pairs, tq, 2 * blk), F32),
            pltpu.VMEM((n_pairs, tq, LANES), F32),
        ],
        compiler_params=pltpu.CompilerParams(
            dimension_semantics=("arbitrary", "arbitrary"), vmem_limit_bytes=VMEM_LIMIT),
        name="sbattn",
    )(proj, kT, proj, tri)


def _mix_kernel(x_ref, osb_ref, cb_ref, cc_ref, cx_ref, cch_ref, cxh_ref, qm_ref,
                g0a_ref, g0b_ref, g1a_ref, g1b_ref, g2a_ref, g2b_ref,
                kmT_ref, vm_ref, convw_ref, wsb_ref, wcv_ref, wmm_ref, wo_ref,
                fg_ref, wrh_ref, wrl_ref, br_ref, ltri_ref,
                x1_ref, hf_ref, idx_ref, gate_ref, rank_ref, cnt_ref,
                run_scr, *, tm, tiles_per_seq):
    i = pl.program_id(0)

    @pl.when(i == 0)
    def _():
        run_scr[...] = jnp.zeros_like(run_scr)

    u = cc_ref[...].astype(F32) * cx_ref[...].astype(F32)
    halo = cch_ref[...].astype(F32) * cxh_ref[...].astype(F32)
    halo = jnp.where(i % tiles_per_seq == 0, 0.0, halo)
    prev1 = halo[-1:, :]
    prev2 = halo[-2:-1, :]
    rows = lax.broadcasted_iota(jnp.int32, u.shape, 0)
    u1 = jnp.where(rows == 0, prev1, pltpu.roll(u, 1, 0))
    u2 = jnp.where(rows == 0, prev2, jnp.where(rows == 1, prev1, pltpu.roll(u, 2, 0)))
    cw = convw_ref[...]
    o_conv = cb_ref[...].astype(F32) * (cw[0:1, :] * u2 + cw[1:2, :] * u1 + cw[2:3, :] * u)

    a_mem = None
    for h in range(MEM_HEADS):
        sl = slice(h * MEM_HEAD_DIM, (h + 1) * MEM_HEAD_DIM)
        s = _dot(qm_ref[:, sl], kmT_ref[0, sl, :])
        s = s - jnp.max(s, axis=-1, keepdims=True)
        e = jnp.exp(s)
        oh = _dot(e.astype(BF16), vm_ref[:, sl]) / jnp.sum(e, axis=-1, keepdims=True)
        part = _dot(oh.astype(BF16), wmm_ref[sl, :])
        a_mem = part if a_mem is None else a_mem + part

    a_sb = _dot(osb_ref[...], wsb_ref[...])
    a_cv = _dot(o_conv.astype(BF16), wcv_ref[...])

    half = a_sb.shape[1] // 2
    gates = ((g0a_ref, g1a_ref, g2a_ref), (g0b_ref, g1b_ref, g2b_ref))
    x1 = x_ref[...]
    for c in range(2):
        cs = slice(c * half, (c + 1) * half)
        g0, g1, g2 = (0.5 * jnp.tanh(0.5 * r[...].astype(F32)) + 0.5 for r in gates[c])
        merged = g0 * a_sb[:, cs] + g1 * a_cv[:, cs] + g2 * a_mem[:, cs]
        x1 = x1 + _dot(merged.astype(BF16), wo_ref[cs, :])
    x1_ref[...] = x1

    ms = jnp.mean(x1 * x1, axis=-1, keepdims=True)
    hf = x1 * lax.rsqrt(ms + RMS_EPS) * fg_ref[...]
    hf_ref[...] = hf
    hf_hi, hf_lo = _split_bf16(hf)
    logits = (_dot(hf_hi, wrh_ref[...]) + _dot(hf_lo, wrh_ref[...]) + _dot(hf_hi, wrl_ref[...])
              + br_ref[...])
    lane = lax.broadcasted_iota(jnp.int32, logits.shape, 1)
    vals, idxs = [], []
    l = logits
    for _ in range(TOP_K):
        m = jnp.max(l, axis=-1, keepdims=True)
        ik = jnp.min(jnp.where(l == m, lane, LANES), axis=-1, keepdims=True)
        vals.append(m)
        idxs.append(ik)
        l = jnp.where(lane == ik, -3.0e38, l)
    exps = [jnp.exp(v - vals[0]) for v in vals]
    denom = exps[0] + exps[1] + exps[2] + exps[3]

    sel = jnp.zeros(logits.shape, F32)
    for ik in idxs:
        sel = sel + jnp.where(lane == ik, 1.0, 0.0)
    rank_full = _dot(ltri_ref[...], sel.astype(BF16)) + run_scr[...]
    run_scr[...] = run_scr[...] + jnp.sum(sel, axis=0, keepdims=True)
    cnt_ref[...] = jnp.broadcast_to(run_scr[...], cnt_ref.shape)

    idx_out = jnp.zeros(logits.shape, jnp.int32)
    gate_out = jnp.zeros(logits.shape, F32)
    rank_out = jnp.zeros(logits.shape, F32)
    for k in range(TOP_K):
        rk = jnp.sum(jnp.where(lane == idxs[k], rank_full, 0.0), axis=-1, keepdims=True)
        idx_out = jnp.where(lane == k, idxs[k], idx_out)
        gate_out = jnp.where(lane == k, exps[k] / denom, gate_out)
        rank_out = jnp.where(lane == k, rk, rank_out)
    idx_ref[...] = idx_out
    gate_ref[...] = gate_out
    rank_ref[...] = rank_out.astype(jnp.int32)


def _mixer(x2, osb, proj, kmT, vmem, conv_w, wsb, wcv, wmm, wo, fg, wr_hi, wr_lo, br_pad, ltri,
           seq, mem_tokens, tm):
    t, d = x2.shape
    width = COL_BLOCK
    tiles_per_seq = seq // tm
    halo_rows = 16
    hb = tm // halo_rows
    gate_blocks_per_branch = d // COL_BLOCK

    def colspec(cb):
        return pl.BlockSpec((tm, width), lambda i: (i, cb))

    def halospec(cb):
        return pl.BlockSpec((halo_rows, width), lambda i: (jnp.maximum(i * hb - 1, 0), cb))

    def full(a):
        nd = a.ndim
        return pl.BlockSpec(a.shape, lambda i: (0,) * nd)

    gate_specs = [colspec(CB_GATES + br * gate_blocks_per_branch + c)
                  for br in range(3) for c in range(gate_blocks_per_branch)]
    in_specs = [
        pl.BlockSpec((tm, d), lambda i: (i, 0)),
        pl.BlockSpec((tm, width), lambda i: (i, 0)),
        colspec(CB_CONV_B), colspec(CB_CONV_C), colspec(CB_CONV_X),
        halospec(CB_CONV_C), halospec(CB_CONV_X),
        colspec(CB_MEM_Q),
        *gate_specs,
        pl.BlockSpec((1, width, mem_tokens), lambda i: (i // tiles_per_seq, 0, 0)),
        pl.BlockSpec((mem_tokens, width), lambda i: (i // tiles_per_seq, 0)),
        full(conv_w), full(wsb), full(wcv), full(wmm), full(wo), full(fg),
        full(wr_hi), full(wr_lo), full(br_pad), full(ltri),
    ]
    out_shape = (
        jax.ShapeDtypeStruct((t, d), F32),
        jax.ShapeDtypeStruct((t, d), F32),
        jax.ShapeDtypeStruct((t, LANES), jnp.int32),
        jax.ShapeDtypeStruct((t, LANES), F32),
        jax.ShapeDtypeStruct((t, LANES), jnp.int32),
        jax.ShapeDtypeStruct((8, LANES), F32),
    )
    out_specs = (
        pl.BlockSpec((tm, d), lambda i: (i, 0)),
        pl.BlockSpec((tm, d), lambda i: (i, 0)),
        pl.BlockSpec((tm, LANES), lambda i: (i, 0)),
        pl.BlockSpec((tm, LANES), lambda i: (i, 0)),
        pl.BlockSpec((tm, LANES), lambda i: (i, 0)),
        pl.BlockSpec((8, LANES), lambda i: (0, 0)),
    )
    n_proj_views = 6 + 3 * gate_blocks_per_branch
    return pl.pallas_call(
        functools.partial(_mix_kernel, tm=tm, tiles_per_seq=tiles_per_seq),
        grid=(t // tm,),
        in_specs=in_specs,
        out_specs=out_specs,
        out_shape=out_shape,
        scratch_shapes=[pltpu.VMEM((1, LANES), F32)],
        compiler_params=pltpu.CompilerParams(
            dimension_semantics=("arbitrary",), vmem_limit_bytes=VMEM_LIMIT),
        name="mixer",
    )(x2, osb, *([proj] * n_proj_views), kmT, vmem, conv_w, wsb, wcv, wmm, wo, fg, wr_hi, wr_lo, br_pad, ltri)


DMA_UNROLL = 8


def _start_row_copies(n, make_copy):
    def body(g, c):
        for u in range(DMA_UNROLL):
            make_copy(g * DMA_UNROLL + u).start(priority=u % 2)
        return c

    lax.fori_loop(0, n // DMA_UNROLL, body, 0)


def _dispatch_kernel(dest_ref, pad_ref, hf_ref, xs_hbm, rows_scr, zrow, sems):
    i = pl.program_id(0)
    tm = hf_ref.shape[0]
    n_pad = pad_ref.shape[0]

    @pl.when(i == 0)
    def _():
        zrow[...] = jnp.zeros_like(zrow)
        _start_row_copies(n_pad, lambda j: pltpu.make_async_copy(zrow.at[0], xs_hbm.at[pad_ref[j]], sems.at[1]))

    rows_scr[...] = hf_ref[...].reshape(rows_scr.shape)
    first = i * (tm * TOP_K)

    def token_group(g, c):
        for u in range(DMA_UNROLL // TOP_K):
            r = g * (DMA_UNROLL // TOP_K) + u
            for k in range(TOP_K):
                pltpu.make_async_copy(rows_scr.at[r], xs_hbm.at[dest_ref[first + r * TOP_K + k]],
                                      sems.at[0]).start(priority=k % 2)
        return c

    lax.fori_loop(0, tm * TOP_K // DMA_UNROLL, token_group, 0)
    for _ in range(TOP_K):
        pltpu.make_async_copy(rows_scr, xs_hbm.at[pl.ds(0, tm)], sems.at[0]).wait()

    @pl.when(i == pl.num_programs(0) - 1)
    def _():
        rows = xs_hbm.at[pl.ds(0, n_pad)]
        pltpu.make_async_copy(rows, rows, sems.at[1]).wait()


def _dispatch(dest_flat, pad_dest, hf, n_rows_total, tm):
    t, d = hf.shape
    grid_spec = pltpu.PrefetchScalarGridSpec(
        num_scalar_prefetch=2,
        grid=(t // tm,),
        in_specs=[pl.BlockSpec((tm, d), lambda i, dr, pr: (i, 0))],
        out_specs=pl.BlockSpec(memory_space=pl.ANY),
        scratch_shapes=[pltpu.VMEM((tm, 1, d), F32), pltpu.VMEM((8, 1, d), F32),
                        pltpu.SemaphoreType.DMA((2,))],
    )
    return pl.pallas_call(
        _dispatch_kernel,
        grid_spec=grid_spec,
        out_shape=jax.ShapeDtypeStruct((n_rows_total, 1, d), F32),
        compiler_params=pltpu.CompilerParams(
            dimension_semantics=("arbitrary",), vmem_limit_bytes=VMEM_LIMIT),
        name="dispatch",
    )(dest_flat, pad_dest, hf)


def _expert_kernel(be_ref, nu_ref, slot_ref, nxt_ref, x_ref, wgu_hbm, bgu_ref, wdn_hbm, bdn_ref, y_ref,
                   x2d, wgu_f32, wdn_f32, wgu_bf, wdn_bf, sems):
    i = pl.program_id(0)
    n_used = nu_ref[0]

    def weight_copies(e, slot):
        return (pltpu.make_async_copy(wgu_hbm.at[e], wgu_f32.at[slot], sems.at[0, slot]),
                pltpu.make_async_copy(wdn_hbm.at[e], wdn_f32.at[slot], sems.at[1, slot]))

    first_of_expert = jnp.logical_or(i == 0, be_ref[i] != be_ref[jnp.maximum(i - 1, 0)])
    for slot in range(2):
        @pl.when(jnp.logical_and(jnp.logical_and(first_of_expert, i < n_used), slot_ref[i] == slot))
        def _():
            @pl.when(i == 0)
            def _():
                for cp in weight_copies(be_ref[0], slot):
                    cp.start()

            for cp in weight_copies(be_ref[i], slot):
                cp.wait()
            wgu_bf[...] = wgu_f32[slot].astype(BF16)
            wdn_bf[...] = wdn_f32[slot].astype(BF16)

            @pl.when(nxt_ref[i] >= 0)
            def _():
                for cp in weight_copies(nxt_ref[i], 1 - slot):
                    cp.start()

    @pl.when(i < n_used)
    def _():
        x2d[...] = x_ref[...].reshape(x2d.shape)
        gu = _dot(x2d[...].astype(BF16), wgu_bf[...]) + bgu_ref[0]
        f = gu.shape[1] // 2
        g = jnp.minimum(gu[:, :f], SWIGLU_LIMIT)
        lin = jnp.clip(gu[:, f:], -SWIGLU_LIMIT, SWIGLU_LIMIT)
        act = g * jax.nn.sigmoid(SWIGLU_ALPHA * g) * (lin + 1.0)
        y = _dot(act.astype(BF16), wdn_bf[...]) + bdn_ref[0]
        y_ref[...] = y.reshape(y_ref.shape)

    @pl.when(i >= n_used)
    def _():
        y_ref[...] = jnp.zeros_like(y_ref)


def _experts(block_e, n_used, w_slot, next_e, xs3, n_rows, wgu, bgu, wdn, bdn, tmoe):
    d = xs3.shape[-1]
    e, _, f2 = wgu.shape

    def x_map(i, be, nu, ws, ne):
        return (jnp.minimum(i, jnp.maximum(nu[0] - 1, 0)), 0, 0)

    grid_spec = pltpu.PrefetchScalarGridSpec(
        num_scalar_prefetch=4,
        grid=(n_rows // tmoe,),
        in_specs=[
            pl.BlockSpec((tmoe, 1, d), x_map),
            pl.BlockSpec(memory_space=pl.ANY),
            pl.BlockSpec((1, 1, f2), lambda i, be, nu, ws, ne: (be[i], 0, 0)),
            pl.BlockSpec(memory_space=pl.ANY),
            pl.BlockSpec((1, 1, d), lambda i, be, nu, ws, ne: (be[i], 0, 0)),
        ],
        out_specs=pl.BlockSpec((tmoe, 1, d), lambda i, be, nu, ws, ne: (i, 0, 0)),
        scratch_shapes=[
            pltpu.VMEM((tmoe, d), F32),
            pltpu.VMEM((2, d, f2), F32),
            pltpu.VMEM((2, f2 // 2, d), F32),
            pltpu.VMEM((d, f2), BF16),
            pltpu.VMEM((f2 // 2, d), BF16),
            pltpu.SemaphoreType.DMA((2, 2)),
        ],
    )
    return pl.pallas_call(
        _expert_kernel,
        grid_spec=grid_spec,
        out_shape=jax.ShapeDtypeStruct((n_rows, 1, d), F32),
        compiler_params=pltpu.CompilerParams(
            dimension_semantics=("arbitrary",), vmem_limit_bytes=VMEM_LIMIT),
        name="experts",
    )(block_e, n_used, w_slot, next_e, xs3, wgu, bgu, wdn, bdn)


def _combine_kernel(dest_ref, x1_ref, gate_ref, y_hbm, o_ref, buf0, buf1, y2d, sems):
    i = pl.program_id(0)
    n = pl.num_programs(0)
    rows, _, d = buf0.shape
    tm = rows // TOP_K
    bufs = (buf0, buf1)

    def start_gather(tile, slot):
        first = tile * rows
        _start_row_copies(rows, lambda r: pltpu.make_async_copy(
            y_hbm.at[dest_ref[first + r]], bufs[slot].at[r], sems.at[slot]))

    @pl.when(i == 0)
    def _():
        start_gather(0, 0)

    for par in range(2):
        @pl.when(i % 2 == par)
        def _():
            @pl.when(i + 1 < n)
            def _():
                start_gather(i + 1, 1 - par)

            pltpu.make_async_copy(y_hbm.at[pl.ds(0, rows)], bufs[par], sems.at[par]).wait()
            y2d[...] = bufs[par][...].reshape(rows, d)

    acc = x1_ref[...]
    gate = gate_ref[...]
    for k in range(TOP_K):
        acc = acc + gate[:, k:k + 1] * y2d[k * tm:(k + 1) * tm, :]
    o_ref[...] = acc


def _combine(dest_tiled, x1, gate, y3, tm):
    t, d = x1.shape
    rows = tm * TOP_K
    grid_spec = pltpu.PrefetchScalarGridSpec(
        num_scalar_prefetch=1,
        grid=(t // tm,),
        in_specs=[
            pl.BlockSpec((tm, d), lambda i, dr: (i, 0)),
            pl.BlockSpec((tm, LANES), lambda i, dr: (i, 0)),
            pl.BlockSpec(memory_space=pl.ANY),
        ],
        out_specs=pl.BlockSpec((tm, d), lambda i, dr: (i, 0)),
        scratch_shapes=[
            pltpu.VMEM((rows, 1, d), F32),
            pltpu.VMEM((rows, 1, d), F32),
            pltpu.VMEM((rows, d), F32),
            pltpu.SemaphoreType.DMA((2,)),
        ],
    )
    return pl.pallas_call(
        _combine_kernel,
        grid_spec=grid_spec,
        out_shape=jax.ShapeDtypeStruct((t, d), F32),
        compiler_params=pltpu.CompilerParams(
            dimension_semantics=("arbitrary",), vmem_limit_bytes=VMEM_LIMIT),
        name="combine",
    )(dest_tiled, x1, gate, y3)


def _layer(x, mem, mix_norm_g, w_in, sb_q_norm_g, sb_k_norm_g, conv_w, mem_norm_g, w_mem_kv,
           mem_q_norm_g, mem_k_norm_g, w_br_sb, w_br_conv, w_br_mem, w_o, ffn_norm_g,
           w_router, b_router, w_gate_up, b_gate_up, w_down, b_down):
    b, s, d = x.shape
    m = mem.shape[1]
    t = b * s
    n_in = w_in.shape[1]
    sb_blk = 128
    sb_tq = 256
    tm_proj = 512
    tm_mix = 256
    tmoe = 256
    tm_comb = 128
    tm_disp = 512

    x2 = x.reshape(t, d)
    nm64 = _group_sum_matrix(COL_BLOCK, SB_HEAD_DIM)
    nm128 = _group_sum_matrix(COL_BLOCK, MEM_HEAD_DIM)

    col_gain = jnp.ones((n_in,), F32)
    col_gain = col_gain.at[CB_Q * COL_BLOCK:(CB_Q + 1) * COL_BLOCK].set(
        jnp.tile(sb_q_norm_g, SB_HEADS) * (SB_HEAD_DIM ** -0.5 * LOG2_E))
    col_gain = col_gain.at[CB_K * COL_BLOCK:(CB_K + 1) * COL_BLOCK].set(jnp.tile(sb_k_norm_g, SB_HEADS))
    col_gain = col_gain.at[CB_MEM_Q * COL_BLOCK:(CB_MEM_Q + 1) * COL_BLOCK].set(
        jnp.tile(mem_q_norm_g, MEM_HEADS) * (MEM_HEAD_DIM ** -0.5))

    proj, kT = _inproj(x2, mix_norm_g.reshape(1, d), w_in.astype(BF16),
                       jnp.stack([nm64, nm128]), col_gain.reshape(1, n_in), tm_proj)

    km, vm = _memkv(mem.reshape(b * m, d), mem_norm_g.reshape(1, d), w_mem_kv.astype(BF16), nm128,
                    jnp.tile(mem_k_norm_g, MEM_HEADS).reshape(1, -1))
    kmT = km.reshape(b, m, -1).transpose(0, 2, 1)

    jj = np.arange(sb_blk)
    later = (jj[:, None] >= jj[None, :]).astype(np.float32)
    ones = np.ones((sb_blk, sb_blk), np.float32)
    eye = np.eye(2, dtype=np.float32)
    tri = np.concatenate([np.kron(eye, later), np.kron(eye, ones)], axis=1)
    o_sb = _sb_attention(proj, kT, jnp.asarray(tri, dtype=BF16), b, s, sb_blk, sb_tq)

    wr_pad = jnp.zeros((d, LANES), F32).at[:, :N_EXPERTS].set(w_router)
    wr_hi, wr_lo = _split_bf16(wr_pad)
    br_pad = jnp.full((1, LANES), -1.0e30, F32).at[0, :N_EXPERTS].set(b_router)
    rr = np.arange(tm_mix)
    ltri = jnp.asarray(rr[None, :] < rr[:, None], dtype=BF16)
    x1, hf, idx, gate, rank, cnt = _mixer(
        x2, o_sb, proj, kmT, vm, conv_w, w_br_sb.astype(BF16), w_br_conv.astype(BF16),
        w_br_mem.astype(BF16), w_o.astype(BF16), ffn_norm_g.reshape(1, d), wr_hi, wr_lo, br_pad, ltri,
        s, m, tm_mix)

    a = t * TOP_K
    idx4 = idx[:, :TOP_K]
    counts = cnt[0, :N_EXPERTS].astype(jnp.int32)
    padded = (counts + tmoe - 1) // tmoe * tmoe
    pend = jnp.cumsum(padded)
    pstart = pend - padded
    dest = pstart[idx4] + rank[:, :TOP_K]
    n_blocks = a // tmoe + N_EXPERTS
    n_rows = n_blocks * tmoe
    block_row0 = jnp.arange(n_blocks, dtype=jnp.int32) * tmoe
    block_e = jnp.minimum(jnp.sum((pend[None, :] <= block_row0[:, None]).astype(jnp.int32), axis=1),
                          N_EXPERTS - 1)
    n_used = (pend[-1] // tmoe).astype(jnp.int32).reshape(1)
    slot = jnp.arange(tmoe, dtype=jnp.int32)
    is_pad = slot[None, :] < (padded - counts)[:, None]
    spare = jnp.logical_not(is_pad).reshape(-1).astype(jnp.int32)
    tail_row = pend[-1] + jnp.cumsum(spare) - spare
    pad_dest = jnp.where(is_pad.reshape(-1), ((pstart + counts)[:, None] + slot[None, :]).reshape(-1), tail_row)
    xs3 = _dispatch(dest.reshape(a), pad_dest, hf, n_rows, tm_disp)
    has_rows = counts > 0
    w_slot = ((jnp.cumsum(has_rows.astype(jnp.int32)) - 1) % 2)[block_e]
    later_e = jnp.where(has_rows[None, :] & (jnp.arange(N_EXPERTS)[None, :] > jnp.arange(N_EXPERTS)[:, None]),
                        jnp.arange(N_EXPERTS, dtype=jnp.int32)[None, :], N_EXPERTS)
    next_e = jnp.min(later_e, axis=1)
    next_e = jnp.where(next_e < N_EXPERTS, next_e, -1)[block_e]
    y3 = _experts(block_e, n_used, w_slot.astype(jnp.int32), next_e.astype(jnp.int32), xs3, n_rows,
                  w_gate_up, b_gate_up[:, None, :], w_down, b_down[:, None, :], tmoe)
    dest_tiled = dest.reshape(t // tm_comb, tm_comb, TOP_K).transpose(0, 2, 1).reshape(a)
    out = _combine(dest_tiled, x1, gate, y3, tm_comb)
    return out.reshape(b, s, d)


def kernel(x, mem, mix_norm_g, w_in, sb_q_norm_g, sb_k_norm_g, conv_w, mem_norm_g, w_mem_kv,
           mem_q_norm_g, mem_k_norm_g, w_br_sb, w_br_conv, w_br_mem, w_o, ffn_norm_g,
           w_router, b_router, w_gate_up, b_gate_up, w_down, b_down):
    depth = mix_norm_g.shape[0]
    for l in range(depth):
        x = _layer(x, mem, mix_norm_g[l], w_in[l], sb_q_norm_g[l], sb_k_norm_g[l], conv_w[l],
                   mem_norm_g[l], w_mem_kv[l], mem_q_norm_g[l], mem_k_norm_g[l], w_br_sb[l],
                   w_br_conv[l], w_br_mem[l], w_o[l], ffn_norm_g[l], w_router[l], b_router[l],
                   w_gate_up[l], b_gate_up[l], w_down[l], b_down[l])
    return x
```

```python
import functools

import numpy as np
import jax
import jax.numpy as jnp
from jax import lax
from jax.experimental import pallas as pl
from jax.experimental.pallas import tpu as pltpu

F32 = jnp.float32
BF16 = jnp.bfloat16

RMS_EPS = 1e-6
SB_HEADS = 8
SB_HEAD_DIM = 64
MEM_HEADS = 4
MEM_HEAD_DIM = 128
N_EXPERTS = 32
TOP_K = 4
SWIGLU_LIMIT = 7.0
SWIGLU_ALPHA = 1.702
LOG2_E = 1.4426950408889634

LANES = 128
COL_BLOCK = 512
CB_Q, CB_K, CB_V, CB_CONV_B, CB_CONV_C, CB_CONV_X, CB_MEM_Q, CB_GATES = range(8)

VMEM_LIMIT = 56 * 1024 * 1024


def _dot(a, b):
    return jnp.dot(a, b, preferred_element_type=F32)


def _split_bf16(x):
    hi = x.astype(BF16)
    lo = (x - hi.astype(F32)).astype(BF16)
    return hi, lo


def _group_sum_matrix(width, group):
    idx = np.arange(width) // group
    return jnp.asarray(idx[:, None] == idx[None, :], dtype=BF16)


def _memkv_kernel(mem_ref, g_ref, w_ref, nm_ref, kg_ref, k_ref, v_ref):
    xf = mem_ref[...]
    ms = jnp.mean(xf * xf, axis=-1, keepdims=True)
    h = (xf * lax.rsqrt(ms + RMS_EPS) * g_ref[...]).astype(BF16)
    kv = _dot(h, w_ref[...])
    width = k_ref.shape[-1]
    k = kv[:, :width]
    hi, lo = _split_bf16(k * k)
    ss = _dot(hi, nm_ref[...]) + _dot(lo, nm_ref[...])
    k_ref[...] = (k * lax.rsqrt(ss * (1.0 / MEM_HEAD_DIM) + RMS_EPS) * kg_ref[...]).astype(BF16)
    v_ref[...] = kv[:, width:].astype(BF16)


def _memkv(mem2, g, w_bf, nm128, kg_cols):
    rows, d = mem2.shape
    width = w_bf.shape[1] // 2
    return pl.pallas_call(
        _memkv_kernel,
        out_shape=(jax.ShapeDtypeStruct((rows, width), BF16),
                   jax.ShapeDtypeStruct((rows, width), BF16)),
        compiler_params=pltpu.CompilerParams(vmem_limit_bytes=VMEM_LIMIT),
        name="memkv",
    )(mem2, g, w_bf, nm128, kg_cols)


def _inproj_kernel(x_ref, g_ref, w_ref, nm_ref, cg_ref, o_ref, kT_ref):
    xf = x_ref[...]
    ms = jnp.mean(xf * xf, axis=-1, keepdims=True)
    h = (xf * lax.rsqrt(ms + RMS_EPS) * g_ref[...]).astype(BF16)
    for j in range(w_ref.shape[1] // COL_BLOCK):
        cs = slice(j * COL_BLOCK, (j + 1) * COL_BLOCK)
        p = _dot(h, w_ref[:, cs])
        if j in (CB_Q, CB_K, CB_MEM_Q):
            group, nm = (MEM_HEAD_DIM, nm_ref[1]) if j == CB_MEM_Q else (SB_HEAD_DIM, nm_ref[0])
            ss = _dot((p * p).astype(BF16), nm)
            p = p * lax.rsqrt(ss * (1.0 / group) + RMS_EPS) * cg_ref[:, cs]
        if j == CB_K:
            for c in range(kT_ref.shape[0]):
                kT_ref[c] = p[c * LANES:(c + 1) * LANES, :].T.astype(BF16)
        o_ref[:, cs] = p.astype(BF16)


def _inproj(x2, g, w_bf, nmats, col_gain, tm):
    t, d = x2.shape
    n = w_bf.shape[1]

    def resident(a):
        nd = a.ndim
        return pl.BlockSpec(a.shape, lambda i: (0,) * nd)

    return pl.pallas_call(
        _inproj_kernel,
        grid=(t // tm,),
        in_specs=[pl.BlockSpec((tm, d), lambda i: (i, 0)),
                  resident(g), resident(w_bf), resident(nmats), resident(col_gain)],
        out_specs=(pl.BlockSpec((tm, n), lambda i: (i, 0)),
                   pl.BlockSpec((tm // LANES, COL_BLOCK, LANES), lambda i: (i, 0, 0))),
        out_shape=(jax.ShapeDtypeStruct((t, n), BF16),
                   jax.ShapeDtypeStruct((t // LANES, COL_BLOCK, LANES), BF16)),
        compiler_params=pltpu.CompilerParams(
            dimension_semantics=("arbitrary",), vmem_limit_bytes=VMEM_LIMIT),
        name="inproj",
    )(x2, g, w_bf, nmats, col_gain)


def _sb_kernel(q_ref, kT_ref, v_ref, tri_ref, o_ref, carry_scr, acc_scr, *, blk):
    i = pl.program_id(1)
    n_pairs = SB_HEADS // 2
    pair_w = 2 * blk
    tq = q_ref.shape[0]
    kpq = tq // blk
    lane = lax.broadcasted_iota(jnp.int32, (1, LANES), 1)
    row = lax.broadcasted_iota(jnp.int32, (tq, pair_w), 0)
    col = lax.broadcasted_iota(jnp.int32, (tq, pair_w), 1)
    key_in_block = jnp.where(col >= blk, col - blk, col)
    head_dim = lax.broadcasted_iota(jnp.int32, (LANES, blk), 0)

    pairs = range(n_pairs)

    def key_block(j, carries, accs, diagonal):
        start = pl.multiple_of(j * blk, blk)
        carries, accs = list(carries), list(accs)
        if diagonal:
            causal = key_in_block + (j - i * kpq) * blk < row
        tri = tri_ref[...]
        zs = []
        for p in pairs:
            kTp = kT_ref[j, p * LANES:(p + 1) * LANES, :]
            zero = jnp.zeros_like(kTp)
            kbd = jnp.concatenate([jnp.where(head_dim < SB_HEAD_DIM, kTp, zero),
                                   jnp.where(head_dim >= SB_HEAD_DIM, kTp, zero)], axis=1)
            zs.append(_dot(q_ref[:, p * LANES:(p + 1) * LANES], kbd))
        costs = []
        for p in pairs:
            cost = jnp.maximum(zs[p], 0.0) + jnp.log2(1.0 + jnp.exp2(-jnp.abs(zs[p])))
            costs.append(jnp.where(causal, cost, 0.0) if diagonal else cost)
        rs = [_dot(costs[p].astype(BF16), tri) for p in pairs]
        for p in pairs:
            w = jnp.exp2(zs[p] - (rs[p][:, :pair_w] + carries[p]))
            if diagonal:
                w = jnp.where(causal, w, 0.0)
            vp = v_ref[pl.ds(start, blk), p * LANES:(p + 1) * LANES]
            zero = jnp.zeros_like(vp)
            vbd = jnp.concatenate([jnp.where(lane < SB_HEAD_DIM, vp, zero),
                                   jnp.where(lane >= SB_HEAD_DIM, vp, zero)], axis=0)
            accs[p] = accs[p] + _dot(w.astype(BF16), vbd)
            total = jnp.concatenate([jnp.broadcast_to(rs[p][:, a * blk:a * blk + 1], (tq, blk)) for a in range(2)],
                                    axis=1)
            carries[p] = carries[p] + total
        return carries, accs

    def save(carries, accs):
        for p in pairs:
            carry_scr[p] = carries[p]
            acc_scr[p] = accs[p]

    carries = [jnp.zeros((tq, pair_w), F32)] * n_pairs
    accs = [jnp.zeros((tq, LANES), F32)] * n_pairs
    for d in reversed(range(kpq)):
        carries, accs = key_block(i * kpq + d, carries, accs, True)
    save(carries, accs)

    def body(t, c):
        carries = [carry_scr[p] for p in pairs]
        accs = [acc_scr[p] for p in pairs]
        for u in range(kpq):
            carries, accs = key_block((i - t) * kpq - 1 - u, carries, accs, False)
        save(carries, accs)
        return c

    lax.fori_loop(0, i, body, 0)

    for p in pairs:
        o_ref[:, p * LANES:(p + 1) * LANES] = acc_scr[p].astype(o_ref.dtype)


def _sb_attention(proj, kT, tri, batch, seq, blk, tq):
    t = proj.shape[0]
    nq = seq // tq
    nk = seq // blk
    width = SB_HEADS * SB_HEAD_DIM
    n_pairs = SB_HEADS // 2
    return pl.pallas_call(
        functools.partial(_sb_kernel, blk=blk),
        grid=(batch, nq),
        in_specs=[
            pl.BlockSpec((tq, width), lambda b, i: (b * nq + i, CB_Q)),
            pl.BlockSpec((nk, width, blk), lambda b, i: (b, 0, 0)),
            pl.BlockSpec((seq, width), lambda b, i: (b, CB_V)),
            pl.BlockSpec(tri.shape, lambda b, i: (0, 0)),
        ],
        out_specs=pl.BlockSpec((tq, width), lambda b, i: (b * nq + i, 0)),
        out_shape=jax.ShapeDtypeStruct((t, width), BF16),
        scratch_shapes=[
            pltpu.VMEM((n_pairs, tq, 2 * blk), F32),
            pltpu.VMEM((n_pairs, tq, LANES), F32),
        ],
        compiler_params=pltpu.CompilerParams(
            dimension_semantics=("arbitrary", "arbitrary"), vmem_limit_bytes=VMEM_LIMIT),
        name="sbattn",
    )(proj, kT, proj, tri)


def _mix_kernel(x_ref, osb_ref, cb_ref, cc_ref, cx_ref, cch_ref, cxh_ref, qm_ref,
                g0a_ref, g0b_ref, g1a_ref, g1b_ref, g2a_ref, g2b_ref,
                kmT_ref, vm_ref, convw_ref, wsb_ref, wcv_ref, wmm_ref, wo_ref,
                fg_ref, wrh_ref, wrl_ref, br_ref, ltri_ref,
                x1_ref, hf_ref, idx_ref, gate_ref, rank_ref, cnt_ref,
                run_scr, *, tm, tiles_per_seq):
    i = pl.program_id(0)

    @pl.when(i == 0)
    def _():
        run_scr[...] = jnp.zeros_like(run_scr)

    u = cc_ref[...].astype(F32) * cx_ref[...].astype(F32)
    halo = cch_ref[...].astype(F32) * cxh_ref[...].astype(F32)
    halo = jnp.where(i % tiles_per_seq == 0, 0.0, halo)
    prev1 = halo[-1:, :]
    prev2 = halo[-2:-1, :]
    rows = lax.broadcasted_iota(jnp.int32, u.shape, 0)
    u1 = jnp.where(rows == 0, prev1, pltpu.roll(u, 1, 0))
    u2 = jnp.where(rows == 0, prev2, jnp.where(rows == 1, prev1, pltpu.roll(u, 2, 0)))
    cw = convw_ref[...]
    o_conv = cb_ref[...].astype(F32) * (cw[0:1, :] * u2 + cw[1:2, :] * u1 + cw[2:3, :] * u)

    a_mem = None
    for h in range(MEM_HEADS):
        sl = slice(h * MEM_HEAD_DIM, (h + 1) * MEM_HEAD_DIM)
        s = _dot(qm_ref[:, sl], kmT_ref[0, sl, :])
        s = s - jnp.max(s, axis=-1, keepdims=True)
        e = jnp.exp(s)
        oh = _dot(e.astype(BF16), vm_ref[:, sl]) / jnp.sum(e, axis=-1, keepdims=True)
        part = _dot(oh.astype(BF16), wmm_ref[sl, :])
        a_mem = part if a_mem is None else a_mem + part

    a_sb = _dot(osb_ref[...], wsb_ref[...])
    a_cv = _dot(o_conv.astype(BF16), wcv_ref[...])

    half = a_sb.shape[1] // 2
    gates = ((g0a_ref, g1a_ref, g2a_ref), (g0b_ref, g1b_ref, g2b_ref))
    x1 = x_ref[...]
    for c in range(2):
        cs = slice(c * half, (c + 1) * half)
        g0, g1, g2 = (0.5 * jnp.tanh(0.5 * r[...].astype(F32)) + 0.5 for r in gates[c])
        merged = g0 * a_sb[:, cs] + g1 * a_cv[:, cs] + g2 * a_mem[:, cs]
        x1 = x1 + _dot(merged.astype(BF16), wo_ref[cs, :])
    x1_ref[...] = x1

    ms = jnp.mean(x1 * x1, axis=-1, keepdims=True)
    hf = x1 * lax.rsqrt(ms + RMS_EPS) * fg_ref[...]
    hf_ref[...] = hf
    hf_hi, hf_lo = _split_bf16(hf)
    logits = (_dot(hf_hi, wrh_ref[...]) + _dot(hf_lo, wrh_ref[...]) + _dot(hf_hi, wrl_ref[...])
              + br_ref[...])
    lane = lax.broadcasted_iota(jnp.int32, logits.shape, 1)
    vals, idxs = [], []
    l = logits
    for _ in range(TOP_K):
        m = jnp.max(l, axis=-1, keepdims=True)
        ik = jnp.min(jnp.where(l == m, lane, LANES), axis=-1, keepdims=True)
        vals.append(m)
        idxs.append(ik)
        l = jnp.where(lane == ik, -3.0e38, l)
    exps = [jnp.exp(v - vals[0]) for v in vals]
    denom = exps[0] + exps[1] + exps[2] + exps[3]

    sel = jnp.zeros(logits.shape, F32)
    for ik in idxs:
        sel = sel + jnp.where(lane == ik, 1.0, 0.0)
    rank_full = _dot(ltri_ref[...], sel.astype(BF16)) + run_scr[...]
    run_scr[...] = run_scr[...] + jnp.sum(sel, axis=0, keepdims=True)
    cnt_ref[...] = jnp.broadcast_to(run_scr[...], cnt_ref.shape)

    idx_out = jnp.zeros(logits.shape, jnp.int32)
    gate_out = jnp.zeros(logits.shape, F32)
    rank_out = jnp.zeros(logits.shape, F32)
    for k in range(TOP_K):
        rk = jnp.sum(jnp.where(lane == idxs[k], rank_full, 0.0), axis=-1, keepdims=True)
        idx_out = jnp.where(lane == k, idxs[k], idx_out)
        gate_out = jnp.where(lane == k, exps[k] / denom, gate_out)
        rank_out = jnp.where(lane == k, rk, rank_out)
    idx_ref[...] = idx_out
    gate_ref[...] = gate_out
    rank_ref[...] = rank_out.astype(jnp.int32)


def _mixer(x2, osb, proj, kmT, vmem, conv_w, wsb, wcv, wmm, wo, fg, wr_hi, wr_lo, br_pad, ltri,
           seq, mem_tokens, tm):
    t, d = x2.shape
    width = COL_BLOCK
    tiles_per_seq = seq // tm
    halo_rows = 16
    hb = tm // halo_rows
    gate_blocks_per_branch = d // COL_BLOCK

    def colspec(cb):
        return pl.BlockSpec((tm, width), lambda i: (i, cb))

    def halospec(cb):
        return pl.BlockSpec((halo_rows, width), lambda i: (jnp.maximum(i * hb - 1, 0), cb))

    def full(a):
        nd = a.ndim
        return pl.BlockSpec(a.shape, lambda i: (0,) * nd)

    gate_specs = [colspec(CB_GATES + br * gate_blocks_per_branch + c)
                  for br in range(3) for c in range(gate_blocks_per_branch)]
    in_specs = [
        pl.BlockSpec((tm, d), lambda i: (i, 0)),
        pl.BlockSpec((tm, width), lambda i: (i, 0)),
        colspec(CB_CONV_B), colspec(CB_CONV_C), colspec(CB_CONV_X),
        halospec(CB_CONV_C), halospec(CB_CONV_X),
        colspec(CB_MEM_Q),
        *gate_specs,
        pl.BlockSpec((1, width, mem_tokens), lambda i: (i // tiles_per_seq, 0, 0)),
        pl.BlockSpec((mem_tokens, width), lambda i: (i // tiles_per_seq, 0)),
        full(conv_w), full(wsb), full(wcv), full(wmm), full(wo), full(fg),
        full(wr_hi), full(wr_lo), full(br_pad), full(ltri),
    ]
    out_shape = (
        jax.ShapeDtypeStruct((t, d), F32),
        jax.ShapeDtypeStruct((t, d), F32),
        jax.ShapeDtypeStruct((t, LANES), jnp.int32),
        jax.ShapeDtypeStruct((t, LANES), F32),
        jax.ShapeDtypeStruct((t, LANES), jnp.int32),
        jax.ShapeDtypeStruct((8, LANES), F32),
    )
    out_specs = (
        pl.BlockSpec((tm, d), lambda i: (i, 0)),
        pl.BlockSpec((tm, d), lambda i: (i, 0)),
        pl.BlockSpec((tm, LANES), lambda i: (i, 0)),
        pl.BlockSpec((tm, LANES), lambda i: (i, 0)),
        pl.BlockSpec((tm, LANES), lambda i: (i, 0)),
        pl.BlockSpec((8, LANES), lambda i: (0, 0)),
    )
    n_proj_views = 6 + 3 * gate_blocks_per_branch
    return pl.pallas_call(
        functools.partial(_mix_kernel, tm=tm, tiles_per_seq=tiles_per_seq),
        grid=(t // tm,),
        in_specs=in_specs,
        out_specs=out_specs,
        out_shape=out_shape,
        scratch_shapes=[pltpu.VMEM((1, LANES), F32)],
        compiler_params=pltpu.CompilerParams(
            dimension_semantics=("arbitrary",), vmem_limit_bytes=VMEM_LIMIT),
        name="mixer",
    )(x2, osb, *([proj] * n_proj_views), kmT, vmem, conv_w, wsb, wcv, wmm, wo, fg, wr_hi, wr_lo, br_pad, ltri)


DMA_UNROLL = 8


def _start_row_copies(n, make_copy):
    def body(g, c):
        for u in range(DMA_UNROLL):
            make_copy(g * DMA_UNROLL + u).start(priority=u % 2)
        return c

    lax.fori_loop(0, n // DMA_UNROLL, body, 0)


def _dispatch_kernel(dest_ref, pad_ref, hf_ref, xs_hbm, rows_scr, zrow, sems):
    i = pl.program_id(0)
    tm = hf_ref.shape[0]
    n_pad = pad_ref.shape[0]

    @pl.when(i == 0)
    def _():
        zrow[...] = jnp.zeros_like(zrow)
        _start_row_copies(n_pad, lambda j: pltpu.make_async_copy(zrow.at[0], xs_hbm.at[pad_ref[j]], sems.at[1]))

    rows_scr[...] = hf_ref[...].reshape(rows_scr.shape)
    first = i * (tm * TOP_K)

    def token_group(g, c):
        for u in range(DMA_UNROLL // TOP_K):
            r = g * (DMA_UNROLL // TOP_K) + u
            for k in range(TOP_K):
                pltpu.make_async_copy(rows_scr.at[r], xs_hbm.at[dest_ref[first + r * TOP_K + k]],
                                      sems.at[0]).start(priority=k % 2)
        return c

    lax.fori_loop(0, tm * TOP_K // DMA_UNROLL, token_group, 0)
    for _ in range(TOP_K):
        pltpu.make_async_copy(rows_scr, xs_hbm.at[pl.ds(0, tm)], sems.at[0]).wait()

    @pl.when(i == pl.num_programs(0) - 1)
    def _():
        rows = xs_hbm.at[pl.ds(0, n_pad)]
        pltpu.make_async_copy(rows, rows, sems.at[1]).wait()


def _dispatch(dest_flat, pad_dest, hf, n_rows_total, tm):
    t, d = hf.shape
    grid_spec = pltpu.PrefetchScalarGridSpec(
        num_scalar_prefetch=2,
        grid=(t // tm,),
        in_specs=[pl.BlockSpec((tm, d), lambda i, dr, pr: (i, 0))],
        out_specs=pl.BlockSpec(memory_space=pl.ANY),
        scratch_shapes=[pltpu.VMEM((tm, 1, d), F32), pltpu.VMEM((8, 1, d), F32),
                        pltpu.SemaphoreType.DMA((2,))],
    )
    return pl.pallas_call(
        _dispatch_kernel,
        grid_spec=grid_spec,
        out_shape=jax.ShapeDtypeStruct((n_rows_total, 1, d), F32),
        compiler_params=pltpu.CompilerParams(
            dimension_semantics=("arbitrary",), vmem_limit_bytes=VMEM_LIMIT),
        name="dispatch",
    )(dest_flat, pad_dest, hf)


def _expert_kernel(be_ref, nu_ref, slot_ref, nxt_ref, x_ref, wgu_hbm, bgu_ref, wdn_hbm, bdn_ref, y_ref,
                   x2d, wgu_f32, wdn_f32, wgu_bf, wdn_bf, sems):
    i = pl.program_id(0)
    n_used = nu_ref[0]

    def weight_copies(e, slot):
        return (pltpu.make_async_copy(wgu_hbm.at[e], wgu_f32.at[slot], sems.at[0, slot]),
                pltpu.make_async_copy(wdn_hbm.at[e], wdn_f32.at[slot], sems.at[1, slot]))

    first_of_expert = jnp.logical_or(i == 0, be_ref[i] != be_ref[jnp.maximum(i - 1, 0)])
    for slot in range(2):
        @pl.when(jnp.logical_and(jnp.logical_and(first_of_expert, i < n_used), slot_ref[i] == slot))
        def _():
            @pl.when(i == 0)
            def _():
                for cp in weight_copies(be_ref[0], slot):
                    cp.start()

            for cp in weight_copies(be_ref[i], slot):
                cp.wait()
            wgu_bf[...] = wgu_f32[slot].astype(BF16)
            wdn_bf[...] = wdn_f32[slot].astype(BF16)

            @pl.when(nxt_ref[i] >= 0)
            def _():
                for cp in weight_copies(nxt_ref[i], 1 - slot):
                    cp.start()

    @pl.when(i < n_used)
    def _():
        x2d[...] = x_ref[...].reshape(x2d.shape)
        gu = _dot(x2d[...].astype(BF16), wgu_bf[...]) + bgu_ref[0]
        f = gu.shape[1] // 2
        g = jnp.minimum(gu[:, :f], SWIGLU_LIMIT)
        lin = jnp.clip(gu[:, f:], -SWIGLU_LIMIT, SWIGLU_LIMIT)
        act = g * (0.5 * jnp.tanh((0.5 * SWIGLU_ALPHA) * g) + 0.5) * (lin + 1.0)
        y = _dot(act.astype(BF16), wdn_bf[...]) + bdn_ref[0]
        y_ref[...] = y.reshape(y_ref.shape)

    @pl.when(i >= n_used)
    def _():
        y_ref[...] = jnp.zeros_like(y_ref)


def _experts(block_e, n_used, w_slot, next_e, xs3, n_rows, wgu, bgu, wdn, bdn, tmoe):
    d = xs3.shape[-1]
    e, _, f2 = wgu.shape

    def x_map(i, be, nu, ws, ne):
        return (jnp.minimum(i, jnp.maximum(nu[0] - 1, 0)), 0, 0)

    grid_spec = pltpu.PrefetchScalarGridSpec(
        num_scalar_prefetch=4,
        grid=(n_rows // tmoe,),
        in_specs=[
            pl.BlockSpec((tmoe, 1, d), x_map),
            pl.BlockSpec(memory_space=pl.ANY),
            pl.BlockSpec((1, 1, f2), lambda i, be, nu, ws, ne: (be[i], 0, 0)),
            pl.BlockSpec(memory_space=pl.ANY),
            pl.BlockSpec((1, 1, d), lambda i, be, nu, ws, ne: (be[i], 0, 0)),
        ],
        out_specs=pl.BlockSpec((tmoe, 1, d), lambda i, be, nu, ws, ne: (i, 0, 0)),
        scratch_shapes=[
            pltpu.VMEM((tmoe, d), F32),
            pltpu.VMEM((2, d, f2), F32),
            pltpu.VMEM((2, f2 // 2, d), F32),
            pltpu.VMEM((d, f2), BF16),
            pltpu.VMEM((f2 // 2, d), BF16),
            pltpu.SemaphoreType.DMA((2, 2)),
        ],
    )
    return pl.pallas_call(
        _expert_kernel,
        grid_spec=grid_spec,
        out_shape=jax.ShapeDtypeStruct((n_rows, 1, d), F32),
        compiler_params=pltpu.CompilerParams(
            dimension_semantics=("arbitrary",), vmem_limit_bytes=VMEM_LIMIT),
        name="experts",
    )(block_e, n_used, w_slot, next_e, xs3, wgu, bgu, wdn, bdn)


def _combine_kernel(dest_ref, x1_ref, gate_ref, y_hbm, o_ref, buf0, buf1, y2d, sems):
    i = pl.program_id(0)
    n = pl.num_programs(0)
    rows, _, d = buf0.shape
    tm = rows // TOP_K
    bufs = (buf0, buf1)

    def start_gather(tile, slot):
        first = tile * rows
        _start_row_copies(rows, lambda r: pltpu.make_async_copy(
            y_hbm.at[dest_ref[first + r]], bufs[slot].at[r], sems.at[slot]))

    @pl.when(i == 0)
    def _():
        start_gather(0, 0)

    for par in range(2):
        @pl.when(i % 2 == par)
        def _():
            @pl.when(i + 1 < n)
            def _():
                start_gather(i + 1, 1 - par)

            pltpu.make_async_copy(y_hbm.at[pl.ds(0, rows)], bufs[par], sems.at[par]).wait()
            y2d[...] = bufs[par][...].reshape(rows, d)

    acc = x1_ref[...]
    gate = gate_ref[...]
    for k in range(TOP_K):
        acc = acc + gate[:, k:k + 1] * y2d[k * tm:(k + 1) * tm, :]
    o_ref[...] = acc


def _combine(dest_tiled, x1, gate, y3, tm):
    t, d = x1.shape
    rows = tm * TOP_K
    grid_spec = pltpu.PrefetchScalarGridSpec(
        num_scalar_prefetch=1,
        grid=(t // tm,),
        in_specs=[
            pl.BlockSpec((tm, d), lambda i, dr: (i, 0)),
            pl.BlockSpec((tm, LANES), lambda i, dr: (i, 0)),
            pl.BlockSpec(memory_space=pl.ANY),
        ],
        out_specs=pl.BlockSpec((tm, d), lambda i, dr: (i, 0)),
        scratch_shapes=[
            pltpu.VMEM((rows, 1, d), F32),
            pltpu.VMEM((rows, 1, d), F32),
            pltpu.VMEM((rows, d), F32),
            pltpu.SemaphoreType.DMA((2,)),
        ],
    )
    return pl.pallas_call(
        _combine_kernel,
        grid_spec=grid_spec,
        out_shape=jax.ShapeDtypeStruct((t, d), F32),
        compiler_params=pltpu.CompilerParams(
            dimension_semantics=("arbitrary",), vmem_limit_bytes=VMEM_LIMIT),
        name="combine",
    )(dest_tiled, x1, gate, y3)


def _layer(x, mem, mix_norm_g, w_in, sb_q_norm_g, sb_k_norm_g, conv_w, mem_norm_g, w_mem_kv,
           mem_q_norm_g, mem_k_norm_g, w_br_sb, w_br_conv, w_br_mem, w_o, ffn_norm_g,
           w_router, b_router, w_gate_up, b_gate_up, w_down, b_down):
    b, s, d = x.shape
    m = mem.shape[1]
    t = b * s
    n_in = w_in.shape[1]
    sb_blk = 128
    sb_tq = 256
    tm_proj = 512
    tm_mix = 512
    tmoe = 256
    tm_comb = 256
    tm_disp = 512

    x2 = x.reshape(t, d)
    nm64 = _group_sum_matrix(COL_BLOCK, SB_HEAD_DIM)
    nm128 = _group_sum_matrix(COL_BLOCK, MEM_HEAD_DIM)

    col_gain = jnp.ones((n_in,), F32)
    col_gain = col_gain.at[CB_Q * COL_BLOCK:(CB_Q + 1) * COL_BLOCK].set(
        jnp.tile(sb_q_norm_g, SB_HEADS) * (SB_HEAD_DIM ** -0.5 * LOG2_E))
    col_gain = col_gain.at[CB_K * COL_BLOCK:(CB_K + 1) * COL_BLOCK].set(jnp.tile(sb_k_norm_g, SB_HEADS))
    col_gain = col_gain.at[CB_MEM_Q * COL_BLOCK:(CB_MEM_Q + 1) * COL_BLOCK].set(
        jnp.tile(mem_q_norm_g, MEM_HEADS) * (MEM_HEAD_DIM ** -0.5))

    proj, kT = _inproj(x2, mix_norm_g.reshape(1, d), w_in.astype(BF16),
                       jnp.stack([nm64, nm128]), col_gain.reshape(1, n_in), tm_proj)

    km, vm = _memkv(mem.reshape(b * m, d), mem_norm_g.reshape(1, d), w_mem_kv.astype(BF16), nm128,
                    jnp.tile(mem_k_norm_g, MEM_HEADS).reshape(1, -1))
    kmT = km.reshape(b, m, -1).transpose(0, 2, 1)

    jj = np.arange(sb_blk)
    later = (jj[:, None] >= jj[None, :]).astype(np.float32)
    ones = np.ones((sb_blk, sb_blk), np.float32)
    eye = np.eye(2, dtype=np.float32)
    tri = np.kron(eye, later)
    o_sb = _sb_attention(proj, kT, jnp.asarray(tri, dtype=BF16), b, s, sb_blk, sb_tq)

    wr_pad = jnp.zeros((d, LANES), F32).at[:, :N_EXPERTS].set(w_router)
    wr_hi, wr_lo = _split_bf16(wr_pad)
    br_pad = jnp.full((1, LANES), -1.0e30, F32).at[0, :N_EXPERTS].set(b_router)
    rr = np.arange(tm_mix)
    ltri = jnp.asarray(rr[None, :] < rr[:, None], dtype=BF16)
    x1, hf, idx, gate, rank, cnt = _mixer(
        x2, o_sb, proj, kmT, vm, conv_w, w_br_sb.astype(BF16), w_br_conv.astype(BF16),
        w_br_mem.astype(BF16), w_o.astype(BF16), ffn_norm_g.reshape(1, d), wr_hi, wr_lo, br_pad, ltri,
        s, m, tm_mix)

    a = t * TOP_K
    idx4 = idx[:, :TOP_K]
    counts = cnt[0, :N_EXPERTS].astype(jnp.int32)
    padded = (counts + tmoe - 1) // tmoe * tmoe
    pend = jnp.cumsum(padded)
    pstart = pend - padded
    dest = pstart[idx4] + rank[:, :TOP_K]
    n_blocks = a // tmoe + N_EXPERTS
    n_rows = n_blocks * tmoe
    block_row0 = jnp.arange(n_blocks, dtype=jnp.int32) * tmoe
    block_e = jnp.minimum(jnp.sum((pend[None, :] <= block_row0[:, None]).astype(jnp.int32), axis=1),
                          N_EXPERTS - 1)
    n_used = (pend[-1] // tmoe).astype(jnp.int32).reshape(1)
    slot = jnp.arange(tmoe, dtype=jnp.int32)
    is_pad = slot[None, :] < (padded - counts)[:, None]
    spare = jnp.logical_not(is_pad).reshape(-1).astype(jnp.int32)
    tail_row = pend[-1] + jnp.cumsum(spare) - spare
    pad_dest = jnp.where(is_pad.reshape(-1), ((pstart + counts)[:, None] + slot[None, :]).reshape(-1), tail_row)
    xs3 = _dispatch(dest.reshape(a), pad_dest, hf, n_rows, tm_disp)
    has_rows = counts > 0
    w_slot = ((jnp.cumsum(has_rows.astype(jnp.int32)) - 1) % 2)[block_e]
    later_e = jnp.where(has_rows[None, :] & (jnp.arange(N_EXPERTS)[None, :] > jnp.arange(N_EXPERTS)[:, None]),
                        jnp.arange(N_EXPERTS, dtype=jnp.int32)[None, :], N_EXPERTS)
    next_e = jnp.min(later_e, axis=1)
    next_e = jnp.where(next_e < N_EXPERTS, next_e, -1)[block_e]
    y3 = _experts(block_e, n_used, w_slot.astype(jnp.int32), next_e.astype(jnp.int32), xs3, n_rows,
                  w_gate_up, b_gate_up[:, None, :], w_down, b_down[:, None, :], tmoe)
    dest_tiled = dest.reshape(t // tm_comb, tm_comb, TOP_K).transpose(0, 2, 1).reshape(a)
    out = _combine(dest_tiled, x1, gate, y3, tm_comb)
    return out.reshape(b, s, d)


def kernel(x, mem, mix_norm_g, w_in, sb_q_norm_g, sb_k_norm_g, conv_w, mem_norm_g, w_mem_kv,
           mem_q_norm_g, mem_k_norm_g, w_br_sb, w_br_conv, w_br_mem, w_o, ffn_norm_g,
           w_router, b_router, w_gate_up, b_gate_up, w_down, b_down):
    depth = mix_norm_g.shape[0]
    for l in range(depth):
        x = _layer(x, mem, mix_norm_g[l], w_in[l], sb_q_norm_g[l], sb_k_norm_g[l], conv_w[l],
                   mem_norm_g[l], w_mem_kv[l], mem_q_norm_g[l], mem_k_norm_g[l], w_br_sb[l],
                   w_br_conv[l], w_br_mem[l], w_o[l], ffn_norm_g[l], w_router[l], b_router[l],
                   w_gate_up[l], b_gate_up[l], w_down[l], b_down[l])
    return x
```

```python
import functools

import numpy as np
import jax
import jax.numpy as jnp
from jax import lax
from jax.experimental import pallas as pl
from jax.experimental.pallas import tpu as pltpu

F32 = jnp.float32
BF16 = jnp.bfloat16

RMS_EPS = 1e-6
SB_HEADS = 8
SB_HEAD_DIM = 64
MEM_HEADS = 4
MEM_HEAD_DIM = 128
N_EXPERTS = 32
TOP_K = 4
SWIGLU_LIMIT = 7.0
SWIGLU_ALPHA = 1.702
LOG2_E = 1.4426950408889634

LANES = 128
COL_BLOCK = 512
CB_Q, CB_K, CB_V, CB_CONV_B, CB_CONV_C, CB_CONV_X, CB_MEM_Q, CB_GATES = range(8)

VMEM_LIMIT = 56 * 1024 * 1024


def _dot(a, b):
    return jnp.dot(a, b, preferred_element_type=F32)


def _split_bf16(x):
    hi = x.astype(BF16)
    lo = (x - hi.astype(F32)).astype(BF16)
    return hi, lo


def _group_sum_matrix(width, group):
    idx = np.arange(width) // group
    return jnp.asarray(idx[:, None] == idx[None, :], dtype=BF16)


def _memkv_kernel(mem_ref, g_ref, w_ref, nm_ref, kg_ref, k_ref, v_ref):
    xf = mem_ref[...]
    ms = jnp.mean(xf * xf, axis=-1, keepdims=True)
    h = (xf * lax.rsqrt(ms + RMS_EPS) * g_ref[...]).astype(BF16)
    kv = _dot(h, w_ref[...])
    width = k_ref.shape[-1]
    k = kv[:, :width]
    hi, lo = _split_bf16(k * k)
    ss = _dot(hi, nm_ref[...]) + _dot(lo, nm_ref[...])
    k_ref[...] = (k * lax.rsqrt(ss * (1.0 / MEM_HEAD_DIM) + RMS_EPS) * kg_ref[...]).astype(BF16)
    v_ref[...] = kv[:, width:].astype(BF16)


def _memkv(mem2, g, w_bf, nm128, kg_cols):
    rows, d = mem2.shape
    width = w_bf.shape[1] // 2
    return pl.pallas_call(
        _memkv_kernel,
        out_shape=(jax.ShapeDtypeStruct((rows, width), BF16),
                   jax.ShapeDtypeStruct((rows, width), BF16)),
        compiler_params=pltpu.CompilerParams(vmem_limit_bytes=VMEM_LIMIT),
        name="memkv",
    )(mem2, g, w_bf, nm128, kg_cols)


def _inproj_kernel(x_ref, g_ref, w_ref, nm_ref, cg_ref, o_ref, kT_ref):
    xf = x_ref[...]
    ms = jnp.mean(xf * xf, axis=-1, keepdims=True)
    h = (xf * lax.rsqrt(ms + RMS_EPS) * g_ref[...]).astype(BF16)
    for j in range(w_ref.shape[1] // COL_BLOCK):
        cs = slice(j * COL_BLOCK, (j + 1) * COL_BLOCK)
        p = _dot(h, w_ref[:, cs])
        if j in (CB_Q, CB_K, CB_MEM_Q):
            group, nm = (MEM_HEAD_DIM, nm_ref[1]) if j == CB_MEM_Q else (SB_HEAD_DIM, nm_ref[0])
            ss = _dot((p * p).astype(BF16), nm)
            p = p * lax.rsqrt(ss * (1.0 / group) + RMS_EPS) * cg_ref[:, cs]
        if j == CB_K:
            for c in range(kT_ref.shape[0]):
                kT_ref[c] = p[c * LANES:(c + 1) * LANES, :].T.astype(BF16)
        o_ref[:, cs] = p.astype(BF16)


def _inproj(x2, g, w_bf, nmats, col_gain, tm):
    t, d = x2.shape
    n = w_bf.shape[1]

    def resident(a):
        nd = a.ndim
        return pl.BlockSpec(a.shape, lambda i: (0,) * nd)

    return pl.pallas_call(
        _inproj_kernel,
        grid=(t // tm,),
        in_specs=[pl.BlockSpec((tm, d), lambda i: (i, 0)),
                  resident(g), resident(w_bf), resident(nmats), resident(col_gain)],
        out_specs=(pl.BlockSpec((tm, n), lambda i: (i, 0)),
                   pl.BlockSpec((tm // LANES, COL_BLOCK, LANES), lambda i: (i, 0, 0))),
        out_shape=(jax.ShapeDtypeStruct((t, n), BF16),
                   jax.ShapeDtypeStruct((t // LANES, COL_BLOCK, LANES), BF16)),
        compiler_params=pltpu.CompilerParams(
            dimension_semantics=("arbitrary",), vmem_limit_bytes=VMEM_LIMIT),
        name="inproj",
    )(x2, g, w_bf, nmats, col_gain)


def _sb_kernel(stop_ref, q_ref, kT_ref, v_ref, tri_ref, o_ref, carry_scr, acc_scr, *, blk):
    i = pl.program_id(1)
    n_pairs = SB_HEADS // 2
    pair_w = 2 * blk
    tq = q_ref.shape[0]
    kpq = tq // blk
    lane = lax.broadcasted_iota(jnp.int32, (1, LANES), 1)
    row = lax.broadcasted_iota(jnp.int32, (tq, pair_w), 0)
    col = lax.broadcasted_iota(jnp.int32, (tq, pair_w), 1)
    key_in_block = jnp.where(col >= blk, col - blk, col)
    head_dim = lax.broadcasted_iota(jnp.int32, (LANES, blk), 0)

    pairs = range(n_pairs)

    def key_block(j, carries, accs, diagonal):
        start = pl.multiple_of(j * blk, blk)
        carries, accs = list(carries), list(accs)
        if diagonal:
            causal = key_in_block + (j - i * kpq) * blk < row
        tri = tri_ref[...]
        zs = []
        for p in pairs:
            kTp = kT_ref[j, p * LANES:(p + 1) * LANES, :]
            zero = jnp.zeros_like(kTp)
            kbd = jnp.concatenate([jnp.where(head_dim < SB_HEAD_DIM, kTp, zero),
                                   jnp.where(head_dim >= SB_HEAD_DIM, kTp, zero)], axis=1)
            zs.append(_dot(q_ref[:, p * LANES:(p + 1) * LANES], kbd))
        costs = []
        for p in pairs:
            cost = jnp.maximum(zs[p], 0.0) + jnp.log2(1.0 + jnp.exp2(-jnp.abs(zs[p])))
            costs.append(jnp.where(causal, cost, 0.0) if diagonal else cost)
        rs = [_dot(costs[p].astype(BF16), tri) for p in pairs]
        for p in pairs:
            w = jnp.exp2(zs[p] - (rs[p][:, :pair_w] + carries[p]))
            if diagonal:
                w = jnp.where(causal, w, 0.0)
            vp = v_ref[pl.ds(start, blk), p * LANES:(p + 1) * LANES]
            zero = jnp.zeros_like(vp)
            vbd = jnp.concatenate([jnp.where(lane < SB_HEAD_DIM, vp, zero),
                                   jnp.where(lane >= SB_HEAD_DIM, vp, zero)], axis=0)
            accs[p] = accs[p] + _dot(w.astype(BF16), vbd)
            total = jnp.concatenate([jnp.broadcast_to(rs[p][:, a * blk:a * blk + 1], (tq, blk)) for a in range(2)],
                                    axis=1)
            carries[p] = carries[p] + total
        return carries, accs

    def save(carries, accs):
        for p in pairs:
            carry_scr[p] = carries[p]
            acc_scr[p] = accs[p]

    carries = [jnp.zeros((tq, pair_w), F32)] * n_pairs
    accs = [jnp.zeros((tq, LANES), F32)] * n_pairs
    for d in reversed(range(kpq)):
        carries, accs = key_block(i * kpq + d, carries, accs, True)
    save(carries, accs)

    def smallest_carry(carries):
        m = carries[0]
        for p in range(1, n_pairs):
            m = jnp.minimum(m, carries[p])
        return jnp.min(m)

    stop_at = stop_ref[0]

    def more_blocks(state):
        t, smallest = state
        return jnp.logical_and(t < i, smallest < stop_at)

    def body(state):
        t, _ = state
        carries = [carry_scr[p] for p in pairs]
        accs = [acc_scr[p] for p in pairs]
        for u in range(kpq):
            carries, accs = key_block((i - t) * kpq - 1 - u, carries, accs, False)
        save(carries, accs)
        return t + 1, smallest_carry(carries)

    lax.while_loop(more_blocks, body, (jnp.int32(0), smallest_carry(carries)))

    for p in pairs:
        o_ref[:, p * LANES:(p + 1) * LANES] = acc_scr[p].astype(o_ref.dtype)


def _sb_attention(stop_at, proj, kT, tri, batch, seq, blk, tq):
    t = proj.shape[0]
    nq = seq // tq
    nk = seq // blk
    width = SB_HEADS * SB_HEAD_DIM
    n_pairs = SB_HEADS // 2
    return pl.pallas_call(
        functools.partial(_sb_kernel, blk=blk),
        grid=(batch, nq),
        in_specs=[
            pl.BlockSpec(memory_space=pltpu.SMEM),
            pl.BlockSpec((tq, width), lambda b, i: (b * nq + i, CB_Q)),
            pl.BlockSpec((nk, width, blk), lambda b, i: (b, 0, 0)),
            pl.BlockSpec((seq, width), lambda b, i: (b, CB_V)),
            pl.BlockSpec(tri.shape, lambda b, i: (0, 0)),
        ],
        out_specs=pl.BlockSpec((tq, width), lambda b, i: (b * nq + i, 0)),
        out_shape=jax.ShapeDtypeStruct((t, width), BF16),
        scratch_shapes=[
            pltpu.VMEM((n_pairs, tq, 2 * blk), F32),
            pltpu.VMEM((n_pairs, tq, LANES), F32),
        ],
        compiler_params=pltpu.CompilerParams(
            dimension_semantics=("arbitrary", "arbitrary"), vmem_limit_bytes=VMEM_LIMIT),
        name="sbattn",
    )(stop_at, proj, kT, proj, tri)


def _mix_kernel(x_ref, osb_ref, cb_ref, cc_ref, cx_ref, cch_ref, cxh_ref, qm_ref,
                g0a_ref, g0b_ref, g1a_ref, g1b_ref, g2a_ref, g2b_ref,
                kmT_ref, vm_ref, convw_ref, wsb_ref, wcv_ref, wmm_ref, wo_ref,
                fg_ref, wrh_ref, wrl_ref, br_ref, ltri_ref,
                x1_ref, hf_ref, idx_ref, gate_ref, rank_ref, cnt_ref,
                run_scr, *, tm, tiles_per_seq):
    i = pl.program_id(0)

    @pl.when(i == 0)
    def _():
        run_scr[...] = jnp.zeros_like(run_scr)

    u = cc_ref[...].astype(F32) * cx_ref[...].astype(F32)
    halo = cch_ref[...].astype(F32) * cxh_ref[...].astype(F32)
    halo = jnp.where(i % tiles_per_seq == 0, 0.0, halo)
    prev1 = halo[-1:, :]
    prev2 = halo[-2:-1, :]
    rows = lax.broadcasted_iota(jnp.int32, u.shape, 0)
    u1 = jnp.where(rows == 0, prev1, pltpu.roll(u, 1, 0))
    u2 = jnp.where(rows == 0, prev2, jnp.where(rows == 1, prev1, pltpu.roll(u, 2, 0)))
    cw = convw_ref[...]
    o_conv = cb_ref[...].astype(F32) * (cw[0:1, :] * u2 + cw[1:2, :] * u1 + cw[2:3, :] * u)

    a_mem = None
    for h in range(MEM_HEADS):
        sl = slice(h * MEM_HEAD_DIM, (h + 1) * MEM_HEAD_DIM)
        s = _dot(qm_ref[:, sl], kmT_ref[0, sl, :])
        s = s - jnp.max(s, axis=-1, keepdims=True)
        e = jnp.exp(s)
        oh = _dot(e.astype(BF16), vm_ref[:, sl]) / jnp.sum(e, axis=-1, keepdims=True)
        part = _dot(oh.astype(BF16), wmm_ref[sl, :])
        a_mem = part if a_mem is None else a_mem + part

    a_sb = _dot(osb_ref[...], wsb_ref[...])
    a_cv = _dot(o_conv.astype(BF16), wcv_ref[...])

    half = a_sb.shape[1] // 2
    gates = ((g0a_ref, g1a_ref, g2a_ref), (g0b_ref, g1b_ref, g2b_ref))
    x1 = x_ref[...]
    for c in range(2):
        cs = slice(c * half, (c + 1) * half)
        g0, g1, g2 = (0.5 * jnp.tanh(0.5 * r[...].astype(F32)) + 0.5 for r in gates[c])
        merged = g0 * a_sb[:, cs] + g1 * a_cv[:, cs] + g2 * a_mem[:, cs]
        x1 = x1 + _dot(merged.astype(BF16), wo_ref[cs, :])
    x1_ref[...] = x1

    ms = jnp.mean(x1 * x1, axis=-1, keepdims=True)
    hf = x1 * lax.rsqrt(ms + RMS_EPS) * fg_ref[...]
    hf_ref[...] = hf
    hf_hi, hf_lo = _split_bf16(hf)
    logits = (_dot(hf_hi, wrh_ref[...]) + _dot(hf_lo, wrh_ref[...]) + _dot(hf_hi, wrl_ref[...])
              + br_ref[...])
    lane = lax.broadcasted_iota(jnp.int32, logits.shape, 1)
    vals, idxs = [], []
    l = logits
    for _ in range(TOP_K):
        m = jnp.max(l, axis=-1, keepdims=True)
        ik = jnp.min(jnp.where(l == m, lane, LANES), axis=-1, keepdims=True)
        vals.append(m)
        idxs.append(ik)
        l = jnp.where(lane == ik, -3.0e38, l)
    exps = [jnp.exp(v - vals[0]) for v in vals]
    denom = exps[0] + exps[1] + exps[2] + exps[3]

    sel = jnp.zeros(logits.shape, F32)
    for ik in idxs:
        sel = sel + jnp.where(lane == ik, 1.0, 0.0)
    rank_full = _dot(ltri_ref[...], sel.astype(BF16)) + run_scr[...]
    run_scr[...] = run_scr[...] + jnp.sum(sel, axis=0, keepdims=True)
    cnt_ref[...] = jnp.broadcast_to(run_scr[...], cnt_ref.shape)

    idx_out = jnp.zeros(logits.shape, jnp.int32)
    gate_out = jnp.zeros(logits.shape, F32)
    rank_out = jnp.zeros(logits.shape, F32)
    for k in range(TOP_K):
        rk = jnp.sum(jnp.where(lane == idxs[k], rank_full, 0.0), axis=-1, keepdims=True)
        idx_out = jnp.where(lane == k, idxs[k], idx_out)
        gate_out = jnp.where(lane == k, exps[k] / denom, gate_out)
        rank_out = jnp.where(lane == k, rk, rank_out)
    idx_ref[...] = idx_out
    gate_ref[...] = gate_out
    rank_ref[...] = rank_out.astype(jnp.int32)


def _mixer(x2, osb, proj, kmT, vmem, conv_w, wsb, wcv, wmm, wo, fg, wr_hi, wr_lo, br_pad, ltri,
           seq, mem_tokens, tm):
    t, d = x2.shape
    width = COL_BLOCK
    tiles_per_seq = seq // tm
    halo_rows = 16
    hb = tm // halo_rows
    gate_blocks_per_branch = d // COL_BLOCK

    def colspec(cb):
        return pl.BlockSpec((tm, width), lambda i: (i, cb))

    def halospec(cb):
        return pl.BlockSpec((halo_rows, width), lambda i: (jnp.maximum(i * hb - 1, 0), cb))

    def full(a):
        nd = a.ndim
        return pl.BlockSpec(a.shape, lambda i: (0,) * nd)

    gate_specs = [colspec(CB_GATES + br * gate_blocks_per_branch + c)
                  for br in range(3) for c in range(gate_blocks_per_branch)]
    in_specs = [
        pl.BlockSpec((tm, d), lambda i: (i, 0)),
        pl.BlockSpec((tm, width), lambda i: (i, 0)),
        colspec(CB_CONV_B), colspec(CB_CONV_C), colspec(CB_CONV_X),
        halospec(CB_CONV_C), halospec(CB_CONV_X),
        colspec(CB_MEM_Q),
        *gate_specs,
        pl.BlockSpec((1, width, mem_tokens), lambda i: (i // tiles_per_seq, 0, 0)),
        pl.BlockSpec((mem_tokens, width), lambda i: (i // tiles_per_seq, 0)),
        full(conv_w), full(wsb), full(wcv), full(wmm), full(wo), full(fg),
        full(wr_hi), full(wr_lo), full(br_pad), full(ltri),
    ]
    out_shape = (
        jax.ShapeDtypeStruct((t, d), F32),
        jax.ShapeDtypeStruct((t, d), F32),
        jax.ShapeDtypeStruct((t, LANES), jnp.int32),
        jax.ShapeDtypeStruct((t, LANES), F32),
        jax.ShapeDtypeStruct((t, LANES), jnp.int32),
        jax.ShapeDtypeStruct((8, LANES), F32),
    )
    out_specs = (
        pl.BlockSpec((tm, d), lambda i: (i, 0)),
        pl.BlockSpec((tm, d), lambda i: (i, 0)),
        pl.BlockSpec((tm, LANES), lambda i: (i, 0)),
        pl.BlockSpec((tm, LANES), lambda i: (i, 0)),
        pl.BlockSpec((tm, LANES), lambda i: (i, 0)),
        pl.BlockSpec((8, LANES), lambda i: (0, 0)),
    )
    n_proj_views = 6 + 3 * gate_blocks_per_branch
    return pl.pallas_call(
        functools.partial(_mix_kernel, tm=tm, tiles_per_seq=tiles_per_seq),
        grid=(t // tm,),
        in_specs=in_specs,
        out_specs=out_specs,
        out_shape=out_shape,
        scratch_shapes=[pltpu.VMEM((1, LANES), F32)],
        compiler_params=pltpu.CompilerParams(
            dimension_semantics=("arbitrary",), vmem_limit_bytes=VMEM_LIMIT),
        name="mixer",
    )(x2, osb, *([proj] * n_proj_views), kmT, vmem, conv_w, wsb, wcv, wmm, wo, fg, wr_hi, wr_lo, br_pad, ltri)


DMA_UNROLL = 8


def _start_row_copies(n, make_copy):
    def body(g, c):
        for u in range(DMA_UNROLL):
            make_copy(g * DMA_UNROLL + u).start(priority=u % 2)
        return c

    lax.fori_loop(0, n // DMA_UNROLL, body, 0)


def _dispatch_kernel(dest_ref, pad_ref, hf_ref, xs_hbm, rows_scr, zrow, sems):
    i = pl.program_id(0)
    tm = hf_ref.shape[0]
    n_pad = pad_ref.shape[0]

    @pl.when(i == 0)
    def _():
        zrow[...] = jnp.zeros_like(zrow)
        _start_row_copies(n_pad, lambda j: pltpu.make_async_copy(zrow.at[0], xs_hbm.at[pad_ref[j]], sems.at[1]))

    rows_scr[...] = hf_ref[...].reshape(rows_scr.shape)
    first = i * (tm * TOP_K)

    def token_group(g, c):
        for u in range(DMA_UNROLL // TOP_K):
            r = g * (DMA_UNROLL // TOP_K) + u
            for k in range(TOP_K):
                pltpu.make_async_copy(rows_scr.at[r], xs_hbm.at[dest_ref[first + r * TOP_K + k]],
                                      sems.at[0]).start(priority=k % 2)
        return c

    lax.fori_loop(0, tm * TOP_K // DMA_UNROLL, token_group, 0)
    for _ in range(TOP_K):
        pltpu.make_async_copy(rows_scr, xs_hbm.at[pl.ds(0, tm)], sems.at[0]).wait()

    @pl.when(i == pl.num_programs(0) - 1)
    def _():
        rows = xs_hbm.at[pl.ds(0, n_pad)]
        pltpu.make_async_copy(rows, rows, sems.at[1]).wait()


def _dispatch(dest_flat, pad_dest, hf, n_rows_total, tm):
    t, d = hf.shape
    grid_spec = pltpu.PrefetchScalarGridSpec(
        num_scalar_prefetch=2,
        grid=(t // tm,),
        in_specs=[pl.BlockSpec((tm, d), lambda i, dr, pr: (i, 0))],
        out_specs=pl.BlockSpec(memory_space=pl.ANY),
        scratch_shapes=[pltpu.VMEM((tm, 1, d), F32), pltpu.VMEM((8, 1, d), F32),
                        pltpu.SemaphoreType.DMA((2,))],
    )
    return pl.pallas_call(
        _dispatch_kernel,
        grid_spec=grid_spec,
        out_shape=jax.ShapeDtypeStruct((n_rows_total, 1, d), F32),
        compiler_params=pltpu.CompilerParams(
            dimension_semantics=("arbitrary",), vmem_limit_bytes=VMEM_LIMIT),
        name="dispatch",
    )(dest_flat, pad_dest, hf)


def _expert_kernel(be_ref, nu_ref, slot_ref, nxt_ref, x_ref, wgu_hbm, bgu_ref, wdn_hbm, bdn_ref, y_ref,
                   x2d, wgu_f32, wdn_f32, wgu_bf, wdn_bf, sems):
    i = pl.program_id(0)
    n_used = nu_ref[0]

    def weight_copies(e, slot):
        return (pltpu.make_async_copy(wgu_hbm.at[e], wgu_f32.at[slot], sems.at[0, slot]),
                pltpu.make_async_copy(wdn_hbm.at[e], wdn_f32.at[slot], sems.at[1, slot]))

    first_of_expert = jnp.logical_or(i == 0, be_ref[i] != be_ref[jnp.maximum(i - 1, 0)])
    for slot in range(2):
        @pl.when(jnp.logical_and(jnp.logical_and(first_of_expert, i < n_used), slot_ref[i] == slot))
        def _():
            @pl.when(i == 0)
            def _():
                for cp in weight_copies(be_ref[0], slot):
                    cp.start()

            for cp in weight_copies(be_ref[i], slot):
                cp.wait()
            wgu_bf[...] = wgu_f32[slot].astype(BF16)
            wdn_bf[...] = wdn_f32[slot].astype(BF16)

            @pl.when(nxt_ref[i] >= 0)
            def _():
                for cp in weight_copies(nxt_ref[i], 1 - slot):
                    cp.start()

    @pl.when(i < n_used)
    def _():
        x2d[...] = x_ref[...].reshape(x2d.shape)
        gu = _dot(x2d[...].astype(BF16), wgu_bf[...]) + bgu_ref[0]
        f = gu.shape[1] // 2
        g = jnp.minimum(gu[:, :f], SWIGLU_LIMIT)
        lin = jnp.clip(gu[:, f:], -SWIGLU_LIMIT, SWIGLU_LIMIT)
        act = g * (0.5 * jnp.tanh((0.5 * SWIGLU_ALPHA) * g) + 0.5) * (lin + 1.0)
        y = _dot(act.astype(BF16), wdn_bf[...]) + bdn_ref[0]
        y_ref[...] = y.reshape(y_ref.shape)

    @pl.when(i >= n_used)
    def _():
        y_ref[...] = jnp.zeros_like(y_ref)


def _experts(block_e, n_used, w_slot, next_e, xs3, n_rows, wgu, bgu, wdn, bdn, tmoe):
    d = xs3.shape[-1]
    e, _, f2 = wgu.shape

    def x_map(i, be, nu, ws, ne):
        return (jnp.minimum(i, jnp.maximum(nu[0] - 1, 0)), 0, 0)

    grid_spec = pltpu.PrefetchScalarGridSpec(
        num_scalar_prefetch=4,
        grid=(n_rows // tmoe,),
        in_specs=[
            pl.BlockSpec((tmoe, 1, d), x_map),
            pl.BlockSpec(memory_space=pl.ANY),
            pl.BlockSpec((1, 1, f2), lambda i, be, nu, ws, ne: (be[i], 0, 0)),
            pl.BlockSpec(memory_space=pl.ANY),
            pl.BlockSpec((1, 1, d), lambda i, be, nu, ws, ne: (be[i], 0, 0)),
        ],
        out_specs=pl.BlockSpec((tmoe, 1, d), lambda i, be, nu, ws, ne: (i, 0, 0)),
        scratch_shapes=[
            pltpu.VMEM((tmoe, d), F32),
            pltpu.VMEM((2, d, f2), F32),
            pltpu.VMEM((2, f2 // 2, d), F32),
            pltpu.VMEM((d, f2), BF16),
            pltpu.VMEM((f2 // 2, d), BF16),
            pltpu.SemaphoreType.DMA((2, 2)),
        ],
    )
    return pl.pallas_call(
        _expert_kernel,
        grid_spec=grid_spec,
        out_shape=jax.ShapeDtypeStruct((n_rows, 1, d), F32),
        compiler_params=pltpu.CompilerParams(
            dimension_semantics=("arbitrary",), vmem_limit_bytes=VMEM_LIMIT),
        name="experts",
    )(block_e, n_used, w_slot, next_e, xs3, wgu, bgu, wdn, bdn)


def _combine_kernel(dest_ref, x1_ref, gate_ref, y_hbm, o_ref, buf0, buf1, y2d, sems):
    i = pl.program_id(0)
    n = pl.num_programs(0)
    rows, _, d = buf0.shape
    tm = rows // TOP_K
    bufs = (buf0, buf1)

    def start_gather(tile, slot):
        first = tile * rows
        _start_row_copies(rows, lambda r: pltpu.make_async_copy(
            y_hbm.at[dest_ref[first + r]], bufs[slot].at[r], sems.at[slot]))

    @pl.when(i == 0)
    def _():
        start_gather(0, 0)

    for par in range(2):
        @pl.when(i % 2 == par)
        def _():
            @pl.when(i + 1 < n)
            def _():
                start_gather(i + 1, 1 - par)

            pltpu.make_async_copy(y_hbm.at[pl.ds(0, rows)], bufs[par], sems.at[par]).wait()
            y2d[...] = bufs[par][...].reshape(rows, d)

    acc = x1_ref[...]
    gate = gate_ref[...]
    for k in range(TOP_K):
        acc = acc + gate[:, k:k + 1] * y2d[k * tm:(k + 1) * tm, :]
    o_ref[...] = acc


def _combine(dest_tiled, x1, gate, y3, tm):
    t, d = x1.shape
    rows = tm * TOP_K
    grid_spec = pltpu.PrefetchScalarGridSpec(
        num_scalar_prefetch=1,
        grid=(t // tm,),
        in_specs=[
            pl.BlockSpec((tm, d), lambda i, dr: (i, 0)),
            pl.BlockSpec((tm, LANES), lambda i, dr: (i, 0)),
            pl.BlockSpec(memory_space=pl.ANY),
        ],
        out_specs=pl.BlockSpec((tm, d), lambda i, dr: (i, 0)),
        scratch_shapes=[
            pltpu.VMEM((rows, 1, d), F32),
            pltpu.VMEM((rows, 1, d), F32),
            pltpu.VMEM((rows, d), F32),
            pltpu.SemaphoreType.DMA((2,)),
        ],
    )
    return pl.pallas_call(
        _combine_kernel,
        grid_spec=grid_spec,
        out_shape=jax.ShapeDtypeStruct((t, d), F32),
        compiler_params=pltpu.CompilerParams(
            dimension_semantics=("arbitrary",), vmem_limit_bytes=VMEM_LIMIT),
        name="combine",
    )(dest_tiled, x1, gate, y3)


def _layer(x, mem, mix_norm_g, w_in, sb_q_norm_g, sb_k_norm_g, conv_w, mem_norm_g, w_mem_kv,
           mem_q_norm_g, mem_k_norm_g, w_br_sb, w_br_conv, w_br_mem, w_o, ffn_norm_g,
           w_router, b_router, w_gate_up, b_gate_up, w_down, b_down):
    b, s, d = x.shape
    m = mem.shape[1]
    t = b * s
    n_in = w_in.shape[1]
    sb_blk = 128
    sb_tq = 256
    tm_proj = 512
    tm_mix = 512
    tmoe = 256
    tm_comb = 256
    tm_disp = 512

    x2 = x.reshape(t, d)
    nm64 = _group_sum_matrix(COL_BLOCK, SB_HEAD_DIM)
    nm128 = _group_sum_matrix(COL_BLOCK, MEM_HEAD_DIM)

    col_gain = jnp.ones((n_in,), F32)
    col_gain = col_gain.at[CB_Q * COL_BLOCK:(CB_Q + 1) * COL_BLOCK].set(
        jnp.tile(sb_q_norm_g, SB_HEADS) * (SB_HEAD_DIM ** -0.5 * LOG2_E))
    col_gain = col_gain.at[CB_K * COL_BLOCK:(CB_K + 1) * COL_BLOCK].set(jnp.tile(sb_k_norm_g, SB_HEADS))
    col_gain = col_gain.at[CB_MEM_Q * COL_BLOCK:(CB_MEM_Q + 1) * COL_BLOCK].set(
        jnp.tile(mem_q_norm_g, MEM_HEADS) * (MEM_HEAD_DIM ** -0.5))

    proj, kT = _inproj(x2, mix_norm_g.reshape(1, d), w_in.astype(BF16),
                       jnp.stack([nm64, nm128]), col_gain.reshape(1, n_in), tm_proj)

    km, vm = _memkv(mem.reshape(b * m, d), mem_norm_g.reshape(1, d), w_mem_kv.astype(BF16), nm128,
                    jnp.tile(mem_k_norm_g, MEM_HEADS).reshape(1, -1))
    kmT = km.reshape(b, m, -1).transpose(0, 2, 1)

    jj = np.arange(sb_blk)
    later = (jj[:, None] >= jj[None, :]).astype(np.float32)
    ones = np.ones((sb_blk, sb_blk), np.float32)
    eye = np.eye(2, dtype=np.float32)
    tri = np.kron(eye, later)
    z_bound = (SB_HEAD_DIM * (SB_HEAD_DIM ** -0.5 * LOG2_E) * 1.02) * jnp.max(jnp.abs(sb_q_norm_g * sb_k_norm_g))
    stop_at = (151.0 + z_bound).reshape(1).astype(F32)
    o_sb = _sb_attention(stop_at, proj, kT, jnp.asarray(tri, dtype=BF16), b, s, sb_blk, sb_tq)

    wr_pad = jnp.zeros((d, LANES), F32).at[:, :N_EXPERTS].set(w_router)
    wr_hi, wr_lo = _split_bf16(wr_pad)
    br_pad = jnp.full((1, LANES), -1.0e30, F32).at[0, :N_EXPERTS].set(b_router)
    rr = np.arange(tm_mix)
    ltri = jnp.asarray(rr[None, :] < rr[:, None], dtype=BF16)
    x1, hf, idx, gate, rank, cnt = _mixer(
        x2, o_sb, proj, kmT, vm, conv_w, w_br_sb.astype(BF16), w_br_conv.astype(BF16),
        w_br_mem.astype(BF16), w_o.astype(BF16), ffn_norm_g.reshape(1, d), wr_hi, wr_lo, br_pad, ltri,
        s, m, tm_mix)

    a = t * TOP_K
    idx4 = idx[:, :TOP_K]
    counts = cnt[0, :N_EXPERTS].astype(jnp.int32)
    padded = (counts + tmoe - 1) // tmoe * tmoe
    pend = jnp.cumsum(padded)
    pstart = pend - padded
    dest = pstart[idx4] + rank[:, :TOP_K]
    n_blocks = a // tmoe + N_EXPERTS
    n_rows = n_blocks * tmoe
    block_row0 = jnp.arange(n_blocks, dtype=jnp.int32) * tmoe
    block_e = jnp.minimum(jnp.sum((pend[None, :] <= block_row0[:, None]).astype(jnp.int32), axis=1),
                          N_EXPERTS - 1)
    n_used = (pend[-1] // tmoe).astype(jnp.int32).reshape(1)
    slot = jnp.arange(tmoe, dtype=jnp.int32)
    is_pad = slot[None, :] < (padded - counts)[:, None]
    spare = jnp.logical_not(is_pad).reshape(-1).astype(jnp.int32)
    tail_row = pend[-1] + jnp.cumsum(spare) - spare
    pad_dest = jnp.where(is_pad.reshape(-1), ((pstart + counts)[:, None] + slot[None, :]).reshape(-1), tail_row)
    xs3 = _dispatch(dest.reshape(a), pad_dest, hf, n_rows, tm_disp)
    has_rows = counts > 0
    w_slot = ((jnp.cumsum(has_rows.astype(jnp.int32)) - 1) % 2)[block_e]
    later_e = jnp.where(has_rows[None, :] & (jnp.arange(N_EXPERTS)[None, :] > jnp.arange(N_EXPERTS)[:, None]),
                        jnp.arange(N_EXPERTS, dtype=jnp.int32)[None, :], N_EXPERTS)
    next_e = jnp.min(later_e, axis=1)
    next_e = jnp.where(next_e < N_EXPERTS, next_e, -1)[block_e]
    y3 = _experts(block_e, n_used, w_slot.astype(jnp.int32), next_e.astype(jnp.int32), xs3, n_rows,
                  w_gate_up, b_gate_up[:, None, :], w_down, b_down[:, None, :], tmoe)
    dest_tiled = dest.reshape(t // tm_comb, tm_comb, TOP_K).transpose(0, 2, 1).reshape(a)
    out = _combine(dest_tiled, x1, gate, y3, tm_comb)
    return out.reshape(b, s, d)


def kernel(x, mem, mix_norm_g, w_in, sb_q_norm_g, sb_k_norm_g, conv_w, mem_norm_g, w_mem_kv,
           mem_q_norm_g, mem_k_norm_g, w_br_sb, w_br_conv, w_br_mem, w_o, ffn_norm_g,
           w_router, b_router, w_gate_up, b_gate_up, w_down, b_down):
    depth = mix_norm_g.shape[0]
    for l in range(depth):
        x = _layer(x, mem, mix_norm_g[l], w_in[l], sb_q_norm_g[l], sb_k_norm_g[l], conv_w[l],
                   mem_norm_g[l], w_mem_kv[l], mem_q_norm_g[l], mem_k_norm_g[l], w_br_sb[l],
                   w_br_conv[l], w_br_mem[l], w_o[l], ffn_norm_g[l], w_router[l], b_router[l],
                   w_gate_up[l], b_gate_up[l], w_down[l], b_down[l])
    return x
```

```python
import functools

import numpy as np
import jax
import jax.numpy as jnp
from jax import lax
from jax.experimental import pallas as pl
from jax.experimental.pallas import tpu as pltpu

F32 = jnp.float32
BF16 = jnp.bfloat16

RMS_EPS = 1e-6
SB_HEADS = 8
SB_HEAD_DIM = 64
MEM_HEADS = 4
MEM_HEAD_DIM = 128
N_EXPERTS = 32
TOP_K = 4
SWIGLU_LIMIT = 7.0
SWIGLU_ALPHA = 1.702
LOG2_E = 1.4426950408889634

LANES = 128
COL_BLOCK = 512
CB_Q, CB_K, CB_V, CB_CONV_B, CB_CONV_C, CB_CONV_X, CB_MEM_Q, CB_GATES = range(8)

VMEM_LIMIT = 56 * 1024 * 1024


def _dot(a, b):
    return jnp.dot(a, b, preferred_element_type=F32)


def _split_bf16(x):
    hi = x.astype(BF16)
    lo = (x - hi.astype(F32)).astype(BF16)
    return hi, lo


def _group_sum_matrix(width, group):
    idx = np.arange(width) // group
    return jnp.asarray(idx[:, None] == idx[None, :], dtype=BF16)


def _memkv_kernel(mem_ref, g_ref, w_ref, nm_ref, kg_ref, k_ref, v_ref):
    xf = mem_ref[...]
    ms = jnp.mean(xf * xf, axis=-1, keepdims=True)
    h = (xf * lax.rsqrt(ms + RMS_EPS) * g_ref[...]).astype(BF16)
    kv = _dot(h, w_ref[...])
    width = k_ref.shape[-1]
    k = kv[:, :width]
    hi, lo = _split_bf16(k * k)
    ss = _dot(hi, nm_ref[...]) + _dot(lo, nm_ref[...])
    k_ref[...] = (k * lax.rsqrt(ss * (1.0 / MEM_HEAD_DIM) + RMS_EPS) * kg_ref[...]).astype(BF16)
    v_ref[...] = kv[:, width:].astype(BF16)


def _memkv(mem2, g, w_bf, nm128, kg_cols):
    rows, d = mem2.shape
    width = w_bf.shape[1] // 2
    return pl.pallas_call(
        _memkv_kernel,
        out_shape=(jax.ShapeDtypeStruct((rows, width), BF16),
                   jax.ShapeDtypeStruct((rows, width), BF16)),
        compiler_params=pltpu.CompilerParams(vmem_limit_bytes=VMEM_LIMIT),
        name="memkv",
    )(mem2, g, w_bf, nm128, kg_cols)


def _inproj_kernel(x_ref, g_ref, w_ref, nm_ref, cg_ref, o_ref, kT_ref):
    xf = x_ref[...]
    ms = jnp.mean(xf * xf, axis=-1, keepdims=True)
    h = (xf * lax.rsqrt(ms + RMS_EPS) * g_ref[...]).astype(BF16)
    for j in range(w_ref.shape[1] // COL_BLOCK):
        cs = slice(j * COL_BLOCK, (j + 1) * COL_BLOCK)
        p = _dot(h, w_ref[:, cs])
        if j in (CB_Q, CB_K, CB_MEM_Q):
            group, nm = (MEM_HEAD_DIM, nm_ref[1]) if j == CB_MEM_Q else (SB_HEAD_DIM, nm_ref[0])
            ss = _dot((p * p).astype(BF16), nm)
            p = p * lax.rsqrt(ss * (1.0 / group) + RMS_EPS) * cg_ref[:, cs]
        if j == CB_K:
            for c in range(kT_ref.shape[0]):
                kT_ref[c] = p[c * LANES:(c + 1) * LANES, :].T.astype(BF16)
        o_ref[:, cs] = p.astype(BF16)


def _inproj(x2, g, w_bf, nmats, col_gain, tm):
    t, d = x2.shape
    n = w_bf.shape[1]

    def resident(a):
        nd = a.ndim
        return pl.BlockSpec(a.shape, lambda i: (0,) * nd)

    return pl.pallas_call(
        _inproj_kernel,
        grid=(t // tm,),
        in_specs=[pl.BlockSpec((tm, d), lambda i: (i, 0)),
                  resident(g), resident(w_bf), resident(nmats), resident(col_gain)],
        out_specs=(pl.BlockSpec((tm, n), lambda i: (i, 0)),
                   pl.BlockSpec((tm // LANES, COL_BLOCK, LANES), lambda i: (i, 0, 0))),
        out_shape=(jax.ShapeDtypeStruct((t, n), BF16),
                   jax.ShapeDtypeStruct((t // LANES, COL_BLOCK, LANES), BF16)),
        compiler_params=pltpu.CompilerParams(
            dimension_semantics=("arbitrary",), vmem_limit_bytes=VMEM_LIMIT),
        name="inproj",
    )(x2, g, w_bf, nmats, col_gain)


def _sb_kernel(stop_ref, q_ref, kT_ref, v_ref, tri_ref, o_ref, carry_scr, acc_scr, *, blk):
    i = pl.program_id(1)
    n_pairs = SB_HEADS // 2
    pair_w = 2 * blk
    tq = q_ref.shape[0]
    kpq = tq // blk
    lane = lax.broadcasted_iota(jnp.int32, (1, LANES), 1)
    row = lax.broadcasted_iota(jnp.int32, (tq, pair_w), 0)
    col = lax.broadcasted_iota(jnp.int32, (tq, pair_w), 1)
    key_in_block = jnp.where(col >= blk, col - blk, col)
    head_dim = lax.broadcasted_iota(jnp.int32, (LANES, blk), 0)

    pairs = range(n_pairs)

    def key_block(j, carries, accs, diagonal):
        start = pl.multiple_of(j * blk, blk)
        carries, accs = list(carries), list(accs)
        if diagonal:
            causal = key_in_block + (j - i * kpq) * blk < row
        tri = tri_ref[...]
        zs = []
        for p in pairs:
            kTp = kT_ref[j, p * LANES:(p + 1) * LANES, :]
            zero = jnp.zeros_like(kTp)
            kbd = jnp.concatenate([jnp.where(head_dim < SB_HEAD_DIM, kTp, zero),
                                   jnp.where(head_dim >= SB_HEAD_DIM, kTp, zero)], axis=1)
            zs.append(_dot(q_ref[:, p * LANES:(p + 1) * LANES], kbd))
        costs = []
        for p in pairs:
            cost = jnp.maximum(zs[p], 0.0) + jnp.log2(1.0 + jnp.exp2(-jnp.abs(zs[p])))
            costs.append(jnp.where(causal, cost, 0.0) if diagonal else cost)
        rs = [_dot(costs[p].astype(BF16), tri) for p in pairs]
        for p in pairs:
            w = jnp.exp2(zs[p] - (rs[p][:, :pair_w] + carries[p]))
            if diagonal:
                w = jnp.where(causal, w, 0.0)
            vp = v_ref[pl.ds(start, blk), p * LANES:(p + 1) * LANES]
            zero = jnp.zeros_like(vp)
            vbd = jnp.concatenate([jnp.where(lane < SB_HEAD_DIM, vp, zero),
                                   jnp.where(lane >= SB_HEAD_DIM, vp, zero)], axis=0)
            accs[p] = accs[p] + _dot(w.astype(BF16), vbd)
            total = jnp.concatenate([jnp.broadcast_to(rs[p][:, a * blk:a * blk + 1], (tq, blk)) for a in range(2)],
                                    axis=1)
            carries[p] = carries[p] + total
        return carries, accs

    def save(carries, accs):
        for p in pairs:
            carry_scr[p] = carries[p]
            acc_scr[p] = accs[p]

    carries = [jnp.zeros((tq, pair_w), F32)] * n_pairs
    accs = [jnp.zeros((tq, LANES), F32)] * n_pairs
    for d in reversed(range(kpq)):
        carries, accs = key_block(i * kpq + d, carries, accs, True)
    save(carries, accs)

    def smallest_carry(carries):
        m = carries[0]
        for p in range(1, n_pairs):
            m = jnp.minimum(m, carries[p])
        return jnp.min(m)

    stop_at = stop_ref[0]

    def more_blocks(state):
        t, smallest = state
        return jnp.logical_and(t < i, smallest < stop_at)

    def body(state):
        t, _ = state
        carries = [carry_scr[p] for p in pairs]
        accs = [acc_scr[p] for p in pairs]
        for u in range(kpq):
            carries, accs = key_block((i - t) * kpq - 1 - u, carries, accs, False)
        save(carries, accs)
        return t + 1, smallest_carry(carries)

    lax.while_loop(more_blocks, body, (jnp.int32(0), smallest_carry(carries)))

    for p in pairs:
        o_ref[:, p * LANES:(p + 1) * LANES] = acc_scr[p].astype(o_ref.dtype)


def _sb_attention(stop_at, proj, kT, tri, batch, seq, blk, tq):
    t = proj.shape[0]
    nq = seq // tq
    nk = seq // blk
    width = SB_HEADS * SB_HEAD_DIM
    n_pairs = SB_HEADS // 2
    return pl.pallas_call(
        functools.partial(_sb_kernel, blk=blk),
        grid=(batch, nq),
        in_specs=[
            pl.BlockSpec(memory_space=pltpu.SMEM),
            pl.BlockSpec((tq, width), lambda b, i: (b * nq + i, CB_Q)),
            pl.BlockSpec((nk, width, blk), lambda b, i: (b, 0, 0)),
            pl.BlockSpec((seq, width), lambda b, i: (b, CB_V)),
            pl.BlockSpec(tri.shape, lambda b, i: (0, 0)),
        ],
        out_specs=pl.BlockSpec((tq, width), lambda b, i: (b * nq + i, 0)),
        out_shape=jax.ShapeDtypeStruct((t, width), BF16),
        scratch_shapes=[
            pltpu.VMEM((n_pairs, tq, 2 * blk), F32),
            pltpu.VMEM((n_pairs, tq, LANES), F32),
        ],
        compiler_params=pltpu.CompilerParams(
            dimension_semantics=("arbitrary", "arbitrary"), vmem_limit_bytes=VMEM_LIMIT),
        name="sbattn",
    )(stop_at, proj, kT, proj, tri)


def _mix_kernel(x_ref, osb_ref, cb_ref, cc_ref, cx_ref, cch_ref, cxh_ref, qm_ref,
                g0a_ref, g0b_ref, g1a_ref, g1b_ref, g2a_ref, g2b_ref,
                kmT_ref, vm_ref, convw_ref, wsb_ref, wcv_ref, wmm_ref, wo_ref,
                fg_ref, wrh_ref, wrl_ref, br_ref, ltri_ref,
                x1_ref, hf_ref, idx_ref, gate_ref, rank_ref, cnt_ref,
                run_scr, *, tm, tiles_per_seq):
    i = pl.program_id(0)

    @pl.when(i == 0)
    def _():
        run_scr[...] = jnp.zeros_like(run_scr)

    u = cc_ref[...].astype(F32) * cx_ref[...].astype(F32)
    halo = cch_ref[...].astype(F32) * cxh_ref[...].astype(F32)
    halo = jnp.where(i % tiles_per_seq == 0, 0.0, halo)
    prev1 = halo[-1:, :]
    prev2 = halo[-2:-1, :]
    rows = lax.broadcasted_iota(jnp.int32, u.shape, 0)
    u1 = jnp.where(rows == 0, prev1, pltpu.roll(u, 1, 0))
    u2 = jnp.where(rows == 0, prev2, jnp.where(rows == 1, prev1, pltpu.roll(u, 2, 0)))
    cw = convw_ref[...]
    o_conv = cb_ref[...].astype(F32) * (cw[0:1, :] * u2 + cw[1:2, :] * u1 + cw[2:3, :] * u)

    a_mem = None
    for h in range(MEM_HEADS):
        sl = slice(h * MEM_HEAD_DIM, (h + 1) * MEM_HEAD_DIM)
        s = _dot(qm_ref[:, sl], kmT_ref[0, sl, :])
        s = s - jnp.max(s, axis=-1, keepdims=True)
        e = jnp.exp(s)
        oh = _dot(e.astype(BF16), vm_ref[:, sl]) / jnp.sum(e, axis=-1, keepdims=True)
        part = _dot(oh.astype(BF16), wmm_ref[sl, :])
        a_mem = part if a_mem is None else a_mem + part

    a_sb = _dot(osb_ref[...], wsb_ref[...])
    a_cv = _dot(o_conv.astype(BF16), wcv_ref[...])

    half = a_sb.shape[1] // 2
    gates = ((g0a_ref, g1a_ref, g2a_ref), (g0b_ref, g1b_ref, g2b_ref))
    x1 = x_ref[...]
    for c in range(2):
        cs = slice(c * half, (c + 1) * half)
        g0, g1, g2 = (0.5 * jnp.tanh(0.5 * r[...].astype(F32)) + 0.5 for r in gates[c])
        merged = g0 * a_sb[:, cs] + g1 * a_cv[:, cs] + g2 * a_mem[:, cs]
        x1 = x1 + _dot(merged.astype(BF16), wo_ref[cs, :])
    x1_ref[...] = x1

    ms = jnp.mean(x1 * x1, axis=-1, keepdims=True)
    hf = x1 * lax.rsqrt(ms + RMS_EPS) * fg_ref[...]
    hf_ref[...] = hf
    hf_hi, hf_lo = _split_bf16(hf)
    logits = (_dot(hf_hi, wrh_ref[...]) + _dot(hf_lo, wrh_ref[...]) + _dot(hf_hi, wrl_ref[...])
              + br_ref[...])
    lane = lax.broadcasted_iota(jnp.int32, logits.shape, 1)
    vals, idxs = [], []
    l = logits
    for _ in range(TOP_K):
        m = jnp.max(l, axis=-1, keepdims=True)
        ik = jnp.min(jnp.where(l == m, lane, LANES), axis=-1, keepdims=True)
        vals.append(m)
        idxs.append(ik)
        l = jnp.where(lane == ik, -3.0e38, l)
    exps = [jnp.exp(v - vals[0]) for v in vals]
    denom = exps[0] + exps[1] + exps[2] + exps[3]

    sel = jnp.zeros(logits.shape, F32)
    for ik in idxs:
        sel = sel + jnp.where(lane == ik, 1.0, 0.0)
    rank_full = _dot(ltri_ref[...], sel.astype(BF16)) + run_scr[...]
    run_scr[...] = run_scr[...] + jnp.sum(sel, axis=0, keepdims=True)
    cnt_ref[...] = jnp.broadcast_to(run_scr[...], cnt_ref.shape)

    idx_out = jnp.zeros(logits.shape, jnp.int32)
    gate_out = jnp.zeros(logits.shape, F32)
    rank_out = jnp.zeros(logits.shape, F32)
    for k in range(TOP_K):
        rk = jnp.sum(jnp.where(lane == idxs[k], rank_full, 0.0), axis=-1, keepdims=True)
        idx_out = jnp.where(lane == k, idxs[k], idx_out)
        gate_out = jnp.where(lane == k, exps[k] / denom, gate_out)
        rank_out = jnp.where(lane == k, rk, rank_out)
    idx_ref[...] = idx_out.T[:idx_ref.shape[0]]
    gate_ref[...] = gate_out
    rank_ref[...] = rank_out.astype(jnp.int32).T[:rank_ref.shape[0]]


def _mixer(x2, osb, proj, kmT, vmem, conv_w, wsb, wcv, wmm, wo, fg, wr_hi, wr_lo, br_pad, ltri,
           seq, mem_tokens, tm):
    t, d = x2.shape
    width = COL_BLOCK
    tiles_per_seq = seq // tm
    halo_rows = 16
    hb = tm // halo_rows
    gate_blocks_per_branch = d // COL_BLOCK

    def colspec(cb):
        return pl.BlockSpec((tm, width), lambda i: (i, cb))

    def halospec(cb):
        return pl.BlockSpec((halo_rows, width), lambda i: (jnp.maximum(i * hb - 1, 0), cb))

    def full(a):
        nd = a.ndim
        return pl.BlockSpec(a.shape, lambda i: (0,) * nd)

    gate_specs = [colspec(CB_GATES + br * gate_blocks_per_branch + c)
                  for br in range(3) for c in range(gate_blocks_per_branch)]
    in_specs = [
        pl.BlockSpec((tm, d), lambda i: (i, 0)),
        pl.BlockSpec((tm, width), lambda i: (i, 0)),
        colspec(CB_CONV_B), colspec(CB_CONV_C), colspec(CB_CONV_X),
        halospec(CB_CONV_C), halospec(CB_CONV_X),
        colspec(CB_MEM_Q),
        *gate_specs,
        pl.BlockSpec((1, width, mem_tokens), lambda i: (i // tiles_per_seq, 0, 0)),
        pl.BlockSpec((mem_tokens, width), lambda i: (i // tiles_per_seq, 0)),
        full(conv_w), full(wsb), full(wcv), full(wmm), full(wo), full(fg),
        full(wr_hi), full(wr_lo), full(br_pad), full(ltri),
    ]
    out_shape = (
        jax.ShapeDtypeStruct((t, d), F32),
        jax.ShapeDtypeStruct((t, d), F32),
        jax.ShapeDtypeStruct((8, t), jnp.int32),
        jax.ShapeDtypeStruct((t, LANES), F32),
        jax.ShapeDtypeStruct((8, t), jnp.int32),
        jax.ShapeDtypeStruct((8, LANES), F32),
    )
    out_specs = (
        pl.BlockSpec((tm, d), lambda i: (i, 0)),
        pl.BlockSpec((tm, d), lambda i: (i, 0)),
        pl.BlockSpec((8, tm), lambda i: (0, i)),
        pl.BlockSpec((tm, LANES), lambda i: (i, 0)),
        pl.BlockSpec((8, tm), lambda i: (0, i)),
        pl.BlockSpec((8, LANES), lambda i: (0, 0)),
    )
    n_proj_views = 6 + 3 * gate_blocks_per_branch
    return pl.pallas_call(
        functools.partial(_mix_kernel, tm=tm, tiles_per_seq=tiles_per_seq),
        grid=(t // tm,),
        in_specs=in_specs,
        out_specs=out_specs,
        out_shape=out_shape,
        scratch_shapes=[pltpu.VMEM((1, LANES), F32)],
        compiler_params=pltpu.CompilerParams(
            dimension_semantics=("arbitrary",), vmem_limit_bytes=VMEM_LIMIT),
        name="mixer",
    )(x2, osb, *([proj] * n_proj_views), kmT, vmem, conv_w, wsb, wcv, wmm, wo, fg, wr_hi, wr_lo, br_pad, ltri)


DMA_UNROLL = 8


def _start_row_copies(n, make_copy):
    def body(g, c):
        for u in range(DMA_UNROLL):
            make_copy(g * DMA_UNROLL + u).start(priority=u % 2)
        return c

    lax.fori_loop(0, n // DMA_UNROLL, body, 0)


def _dispatch_kernel(dest_ref, pad_ref, hf_ref, xs_hbm, rows_scr, zrow, sems):
    i = pl.program_id(0)
    tm = hf_ref.shape[0]
    n_pad = pad_ref.shape[0]

    @pl.when(i == 0)
    def _():
        zrow[...] = jnp.zeros_like(zrow)
        _start_row_copies(n_pad, lambda j: pltpu.make_async_copy(zrow.at[0], xs_hbm.at[pad_ref[j]], sems.at[1]))

    rows_scr[...] = hf_ref[...].reshape(rows_scr.shape)
    n_tok = pl.num_programs(0) * tm
    first = i * tm

    def token_group(g, c):
        for u in range(DMA_UNROLL // TOP_K):
            r = g * (DMA_UNROLL // TOP_K) + u
            for k in range(TOP_K):
                pltpu.make_async_copy(rows_scr.at[r], xs_hbm.at[dest_ref[k * n_tok + first + r]],
                                      sems.at[0]).start(priority=k % 2)
        return c

    lax.fori_loop(0, tm * TOP_K // DMA_UNROLL, token_group, 0)
    for _ in range(TOP_K):
        pltpu.make_async_copy(rows_scr, xs_hbm.at[pl.ds(0, tm)], sems.at[0]).wait()

    @pl.when(i == pl.num_programs(0) - 1)
    def _():
        rows = xs_hbm.at[pl.ds(0, n_pad)]
        pltpu.make_async_copy(rows, rows, sems.at[1]).wait()


def _dispatch(dest_flat, pad_dest, hf, n_rows_total, tm):
    t, d = hf.shape
    grid_spec = pltpu.PrefetchScalarGridSpec(
        num_scalar_prefetch=2,
        grid=(t // tm,),
        in_specs=[pl.BlockSpec((tm, d), lambda i, dr, pr: (i, 0))],
        out_specs=pl.BlockSpec(memory_space=pl.ANY),
        scratch_shapes=[pltpu.VMEM((tm, 1, d), F32), pltpu.VMEM((8, 1, d), F32),
                        pltpu.SemaphoreType.DMA((2,))],
    )
    return pl.pallas_call(
        _dispatch_kernel,
        grid_spec=grid_spec,
        out_shape=jax.ShapeDtypeStruct((n_rows_total, 1, d), F32),
        compiler_params=pltpu.CompilerParams(
            dimension_semantics=("arbitrary",), vmem_limit_bytes=VMEM_LIMIT),
        name="dispatch",
    )(dest_flat, pad_dest, hf)


def _expert_kernel(be_ref, nu_ref, slot_ref, nxt_ref, x_ref, wgu_hbm, bgu_ref, wdn_hbm, bdn_ref, y_ref,
                   x2d, wgu_f32, wdn_f32, wgu_bf, wdn_bf, sems):
    i = pl.program_id(0)
    n_used = nu_ref[0]

    def weight_copies(e, slot):
        return (pltpu.make_async_copy(wgu_hbm.at[e], wgu_f32.at[slot], sems.at[0, slot]),
                pltpu.make_async_copy(wdn_hbm.at[e], wdn_f32.at[slot], sems.at[1, slot]))

    first_of_expert = jnp.logical_or(i == 0, be_ref[i] != be_ref[jnp.maximum(i - 1, 0)])
    for slot in range(2):
        @pl.when(jnp.logical_and(jnp.logical_and(first_of_expert, i < n_used), slot_ref[i] == slot))
        def _():
            @pl.when(i == 0)
            def _():
                for cp in weight_copies(be_ref[0], slot):
                    cp.start()

            for cp in weight_copies(be_ref[i], slot):
                cp.wait()
            wgu_bf[...] = wgu_f32[slot].astype(BF16)
            wdn_bf[...] = wdn_f32[slot].astype(BF16)

            @pl.when(nxt_ref[i] >= 0)
            def _():
                for cp in weight_copies(nxt_ref[i], 1 - slot):
                    cp.start()

    @pl.when(i < n_used)
    def _():
        x2d[...] = x_ref[...].reshape(x2d.shape)
        gu = _dot(x2d[...].astype(BF16), wgu_bf[...]) + bgu_ref[0]
        f = gu.shape[1] // 2
        g = jnp.minimum(gu[:, :f], SWIGLU_LIMIT)
        lin = jnp.clip(gu[:, f:], -SWIGLU_LIMIT, SWIGLU_LIMIT)
        act = g * (0.5 * jnp.tanh((0.5 * SWIGLU_ALPHA) * g) + 0.5) * (lin + 1.0)
        y = _dot(act.astype(BF16), wdn_bf[...]) + bdn_ref[0]
        y_ref[...] = y.reshape(y_ref.shape)

    @pl.when(i >= n_used)
    def _():
        y_ref[...] = jnp.zeros_like(y_ref)


def _experts(block_e, n_used, w_slot, next_e, xs3, n_rows, wgu, bgu, wdn, bdn, tmoe):
    d = xs3.shape[-1]
    e, _, f2 = wgu.shape

    def x_map(i, be, nu, ws, ne):
        return (jnp.minimum(i, jnp.maximum(nu[0] - 1, 0)), 0, 0)

    grid_spec = pltpu.PrefetchScalarGridSpec(
        num_scalar_prefetch=4,
        grid=(n_rows // tmoe,),
        in_specs=[
            pl.BlockSpec((tmoe, 1, d), x_map),
            pl.BlockSpec(memory_space=pl.ANY),
            pl.BlockSpec((1, 1, f2), lambda i, be, nu, ws, ne: (be[i], 0, 0)),
            pl.BlockSpec(memory_space=pl.ANY),
            pl.BlockSpec((1, 1, d), lambda i, be, nu, ws, ne: (be[i], 0, 0)),
        ],
        out_specs=pl.BlockSpec((tmoe, 1, d), lambda i, be, nu, ws, ne: (i, 0, 0)),
        scratch_shapes=[
            pltpu.VMEM((tmoe, d), F32),
            pltpu.VMEM((2, d, f2), F32),
            pltpu.VMEM((2, f2 // 2, d), F32),
            pltpu.VMEM((d, f2), BF16),
            pltpu.VMEM((f2 // 2, d), BF16),
            pltpu.SemaphoreType.DMA((2, 2)),
        ],
    )
    return pl.pallas_call(
        _expert_kernel,
        grid_spec=grid_spec,
        out_shape=jax.ShapeDtypeStruct((n_rows, 1, d), F32),
        compiler_params=pltpu.CompilerParams(
            dimension_semantics=("arbitrary",), vmem_limit_bytes=VMEM_LIMIT),
        name="experts",
    )(block_e, n_used, w_slot, next_e, xs3, wgu, bgu, wdn, bdn)


def _combine_kernel(dest_ref, x1_ref, gate_ref, y_hbm, o_ref, buf0, buf1, y2d, sems):
    i = pl.program_id(0)
    n = pl.num_programs(0)
    rows, _, d = buf0.shape
    tm = rows // TOP_K
    bufs = (buf0, buf1)

    def start_gather(tile, slot):
        first = tile * rows
        _start_row_copies(rows, lambda r: pltpu.make_async_copy(
            y_hbm.at[dest_ref[first + r]], bufs[slot].at[r], sems.at[slot]))

    @pl.when(i == 0)
    def _():
        start_gather(0, 0)

    for par in range(2):
        @pl.when(i % 2 == par)
        def _():
            @pl.when(i + 1 < n)
            def _():
                start_gather(i + 1, 1 - par)

            pltpu.make_async_copy(y_hbm.at[pl.ds(0, rows)], bufs[par], sems.at[par]).wait()
            y2d[...] = bufs[par][...].reshape(rows, d)

    acc = x1_ref[...]
    gate = gate_ref[...]
    for k in range(TOP_K):
        acc = acc + gate[:, k:k + 1] * y2d[k * tm:(k + 1) * tm, :]
    o_ref[...] = acc


def _combine(dest_tiled, x1, gate, y3, tm):
    t, d = x1.shape
    rows = tm * TOP_K
    grid_spec = pltpu.PrefetchScalarGridSpec(
        num_scalar_prefetch=1,
        grid=(t // tm,),
        in_specs=[
            pl.BlockSpec((tm, d), lambda i, dr: (i, 0)),
            pl.BlockSpec((tm, LANES), lambda i, dr: (i, 0)),
            pl.BlockSpec(memory_space=pl.ANY),
        ],
        out_specs=pl.BlockSpec((tm, d), lambda i, dr: (i, 0)),
        scratch_shapes=[
            pltpu.VMEM((rows, 1, d), F32),
            pltpu.VMEM((rows, 1, d), F32),
            pltpu.VMEM((rows, d), F32),
            pltpu.SemaphoreType.DMA((2,)),
        ],
    )
    return pl.pallas_call(
        _combine_kernel,
        grid_spec=grid_spec,
        out_shape=jax.ShapeDtypeStruct((t, d), F32),
        compiler_params=pltpu.CompilerParams(
            dimension_semantics=("arbitrary",), vmem_limit_bytes=VMEM_LIMIT),
        name="combine",
    )(dest_tiled, x1, gate, y3)


def _layer(x, mem, mix_norm_g, w_in, sb_q_norm_g, sb_k_norm_g, conv_w, mem_norm_g, w_mem_kv,
           mem_q_norm_g, mem_k_norm_g, w_br_sb, w_br_conv, w_br_mem, w_o, ffn_norm_g,
           w_router, b_router, w_gate_up, b_gate_up, w_down, b_down):
    b, s, d = x.shape
    m = mem.shape[1]
    t = b * s
    n_in = w_in.shape[1]
    sb_blk = 128
    sb_tq = 256
    tm_proj = 512
    tm_mix = 512
    tmoe = 256
    tm_comb = 256
    tm_disp = 512

    x2 = x.reshape(t, d)
    nm64 = _group_sum_matrix(COL_BLOCK, SB_HEAD_DIM)
    nm128 = _group_sum_matrix(COL_BLOCK, MEM_HEAD_DIM)

    col_gain = jnp.ones((n_in,), F32)
    col_gain = col_gain.at[CB_Q * COL_BLOCK:(CB_Q + 1) * COL_BLOCK].set(
        jnp.tile(sb_q_norm_g, SB_HEADS) * (SB_HEAD_DIM ** -0.5 * LOG2_E))
    col_gain = col_gain.at[CB_K * COL_BLOCK:(CB_K + 1) * COL_BLOCK].set(jnp.tile(sb_k_norm_g, SB_HEADS))
    col_gain = col_gain.at[CB_MEM_Q * COL_BLOCK:(CB_MEM_Q + 1) * COL_BLOCK].set(
        jnp.tile(mem_q_norm_g, MEM_HEADS) * (MEM_HEAD_DIM ** -0.5))

    proj, kT = _inproj(x2, mix_norm_g.reshape(1, d), w_in.astype(BF16),
                       jnp.stack([nm64, nm128]), col_gain.reshape(1, n_in), tm_proj)

    km, vm = _memkv(mem.reshape(b * m, d), mem_norm_g.reshape(1, d), w_mem_kv.astype(BF16), nm128,
                    jnp.tile(mem_k_norm_g, MEM_HEADS).reshape(1, -1))
    kmT = km.reshape(b, m, -1).transpose(0, 2, 1)

    jj = np.arange(sb_blk)
    later = (jj[:, None] >= jj[None, :]).astype(np.float32)
    ones = np.ones((sb_blk, sb_blk), np.float32)
    eye = np.eye(2, dtype=np.float32)
    tri = np.kron(eye, later)
    z_bound = (SB_HEAD_DIM * (SB_HEAD_DIM ** -0.5 * LOG2_E) * 1.02) * jnp.max(jnp.abs(sb_q_norm_g * sb_k_norm_g))
    stop_at = (151.0 + z_bound).reshape(1).astype(F32)
    o_sb = _sb_attention(stop_at, proj, kT, jnp.asarray(tri, dtype=BF16), b, s, sb_blk, sb_tq)

    wr_pad = jnp.zeros((d, LANES), F32).at[:, :N_EXPERTS].set(w_router)
    wr_hi, wr_lo = _split_bf16(wr_pad)
    br_pad = jnp.full((1, LANES), -1.0e30, F32).at[0, :N_EXPERTS].set(b_router)
    rr = np.arange(tm_mix)
    ltri = jnp.asarray(rr[None, :] < rr[:, None], dtype=BF16)
    x1, hf, idx, gate, rank, cnt = _mixer(
        x2, o_sb, proj, kmT, vm, conv_w, w_br_sb.astype(BF16), w_br_conv.astype(BF16),
        w_br_mem.astype(BF16), w_o.astype(BF16), ffn_norm_g.reshape(1, d), wr_hi, wr_lo, br_pad, ltri,
        s, m, tm_mix)

    a = t * TOP_K
    counts = cnt[0, :N_EXPERTS].astype(jnp.int32)
    padded = (counts + tmoe - 1) // tmoe * tmoe
    pend = jnp.cumsum(padded)
    pstart = pend - padded
    dest = pstart[idx[:TOP_K]] + rank[:TOP_K]
    n_blocks = a // tmoe + N_EXPERTS
    n_rows = n_blocks * tmoe
    block_row0 = jnp.arange(n_blocks, dtype=jnp.int32) * tmoe
    block_e = jnp.minimum(jnp.sum((pend[None, :] <= block_row0[:, None]).astype(jnp.int32), axis=1),
                          N_EXPERTS - 1)
    n_used = (pend[-1] // tmoe).astype(jnp.int32).reshape(1)
    slot = jnp.arange(tmoe, dtype=jnp.int32)
    n_pad = padded - counts
    is_pad = slot[None, :] < n_pad[:, None]
    spare_index = (jnp.arange(N_EXPERTS, dtype=jnp.int32) * tmoe - jnp.cumsum(n_pad))[:, None] + slot[None, :]
    pad_dest = jnp.where(is_pad, (pstart + counts)[:, None] + slot[None, :], pend[-1] + spare_index).reshape(-1)
    xs3 = _dispatch(dest.reshape(a), pad_dest, hf, n_rows, tm_disp)
    has_rows = counts > 0
    w_slot = ((jnp.cumsum(has_rows.astype(jnp.int32)) - 1) % 2)[block_e]
    later_e = jnp.where(has_rows[None, :] & (jnp.arange(N_EXPERTS)[None, :] > jnp.arange(N_EXPERTS)[:, None]),
                        jnp.arange(N_EXPERTS, dtype=jnp.int32)[None, :], N_EXPERTS)
    next_e = jnp.min(later_e, axis=1)
    next_e = jnp.where(next_e < N_EXPERTS, next_e, -1)[block_e]
    y3 = _experts(block_e, n_used, w_slot.astype(jnp.int32), next_e.astype(jnp.int32), xs3, n_rows,
                  w_gate_up, b_gate_up[:, None, :], w_down, b_down[:, None, :], tmoe)
    dest_tiled = dest.reshape(TOP_K, t // tm_comb, tm_comb).transpose(1, 0, 2).reshape(a)
    out = _combine(dest_tiled, x1, gate, y3, tm_comb)
    return out.reshape(b, s, d)


def kernel(x, mem, mix_norm_g, w_in, sb_q_norm_g, sb_k_norm_g, conv_w, mem_norm_g, w_mem_kv,
           mem_q_norm_g, mem_k_norm_g, w_br_sb, w_br_conv, w_br_mem, w_o, ffn_norm_g,
           w_router, b_router, w_gate_up, b_gate_up, w_down, b_down):
    depth = mix_norm_g.shape[0]
    for l in range(depth):
        x = _layer(x, mem, mix_norm_g[l], w_in[l], sb_q_norm_g[l], sb_k_norm_g[l], conv_w[l],
                   mem_norm_g[l], w_mem_kv[l], mem_q_norm_g[l], mem_k_norm_g[l], w_br_sb[l],
                   w_br_conv[l], w_br_mem[l], w_o[l], ffn_norm_g[l], w_router[l], b_router[l],
                   w_gate_up[l], b_gate_up[l], w_down[l], b_down[l])
    return x
```

```python
import functools

import numpy as np
import jax
import jax.numpy as jnp
from jax import lax
from jax.experimental import pallas as pl
from jax.experimental.pallas import tpu as pltpu

F32 = jnp.float32
BF16 = jnp.bfloat16

RMS_EPS = 1e-6
SB_HEADS = 8
SB_HEAD_DIM = 64
MEM_HEADS = 4
MEM_HEAD_DIM = 128
N_EXPERTS = 32
TOP_K = 4
SWIGLU_LIMIT = 7.0
SWIGLU_ALPHA = 1.702
LOG2_E = 1.4426950408889634

LANES = 128
COL_BLOCK = 512
CB_Q, CB_K, CB_V, CB_CONV_B, CB_CONV_C, CB_CONV_X, CB_MEM_Q, CB_GATES = range(8)

VMEM_LIMIT = 56 * 1024 * 1024


def _dot(a, b):
    return jnp.dot(a, b, preferred_element_type=F32)


def _split_bf16(x):
    hi = x.astype(BF16)
    lo = (x - hi.astype(F32)).astype(BF16)
    return hi, lo


def _group_sum_matrix(width, group):
    idx = np.arange(width) // group
    return jnp.asarray(idx[:, None] == idx[None, :], dtype=BF16)


def _memkv_kernel(mem_ref, g_ref, w_ref, nm_ref, kg_ref, k_ref, v_ref):
    xf = mem_ref[...]
    ms = jnp.mean(xf * xf, axis=-1, keepdims=True)
    h = (xf * lax.rsqrt(ms + RMS_EPS) * g_ref[...]).astype(BF16)
    kv = _dot(h, w_ref[...])
    width = k_ref.shape[-1]
    k = kv[:, :width]
    hi, lo = _split_bf16(k * k)
    ss = _dot(hi, nm_ref[...]) + _dot(lo, nm_ref[...])
    k_ref[...] = (k * lax.rsqrt(ss * (1.0 / MEM_HEAD_DIM) + RMS_EPS) * kg_ref[...]).astype(BF16)
    v_ref[...] = kv[:, width:].astype(BF16)


def _memkv(mem2, g, w_bf, nm128, kg_cols):
    rows, d = mem2.shape
    width = w_bf.shape[1] // 2
    return pl.pallas_call(
        _memkv_kernel,
        out_shape=(jax.ShapeDtypeStruct((rows, width), BF16),
                   jax.ShapeDtypeStruct((rows, width), BF16)),
        compiler_params=pltpu.CompilerParams(vmem_limit_bytes=VMEM_LIMIT),
        name="memkv",
    )(mem2, g, w_bf, nm128, kg_cols)


def _inproj_kernel(x_ref, g_ref, w_ref, nm_ref, cg_ref, o_ref, kT_ref):
    xf = x_ref[...]
    ms = jnp.mean(xf * xf, axis=-1, keepdims=True)
    h = (xf * lax.rsqrt(ms + RMS_EPS) * g_ref[...]).astype(BF16)
    for j in range(w_ref.shape[1] // COL_BLOCK):
        cs = slice(j * COL_BLOCK, (j + 1) * COL_BLOCK)
        p = _dot(h, w_ref[:, cs])
        if j in (CB_Q, CB_K, CB_MEM_Q):
            group, nm = (MEM_HEAD_DIM, nm_ref[1]) if j == CB_MEM_Q else (SB_HEAD_DIM, nm_ref[0])
            ss = _dot((p * p).astype(BF16), nm)
            p = p * lax.rsqrt(ss * (1.0 / group) + RMS_EPS) * cg_ref[:, cs]
        if j == CB_K:
            for c in range(kT_ref.shape[0]):
                kT_ref[c] = p[c * LANES:(c + 1) * LANES, :].T.astype(BF16)
        o_ref[:, cs] = p.astype(BF16)


def _inproj(x2, g, w_bf, nmats, col_gain, tm):
    t, d = x2.shape
    n = w_bf.shape[1]

    def resident(a):
        nd = a.ndim
        return pl.BlockSpec(a.shape, lambda i: (0,) * nd)

    return pl.pallas_call(
        _inproj_kernel,
        grid=(t // tm,),
        in_specs=[pl.BlockSpec((tm, d), lambda i: (i, 0)),
                  resident(g), resident(w_bf), resident(nmats), resident(col_gain)],
        out_specs=(pl.BlockSpec((tm, n), lambda i: (i, 0)),
                   pl.BlockSpec((tm // LANES, COL_BLOCK, LANES), lambda i: (i, 0, 0))),
        out_shape=(jax.ShapeDtypeStruct((t, n), BF16),
                   jax.ShapeDtypeStruct((t // LANES, COL_BLOCK, LANES), BF16)),
        compiler_params=pltpu.CompilerParams(
            dimension_semantics=("arbitrary",), vmem_limit_bytes=VMEM_LIMIT),
        name="inproj",
    )(x2, g, w_bf, nmats, col_gain)


def _sb_kernel(stop_ref, q_ref, kT_ref, v_ref, tri_ref, o_ref, carry_scr, acc_scr, *, blk):
    i = pl.program_id(1)
    n_pairs = SB_HEADS // 2
    pair_w = 2 * blk
    tq = q_ref.shape[0]
    kpq = tq // blk
    lane = lax.broadcasted_iota(jnp.int32, (1, LANES), 1)
    row = lax.broadcasted_iota(jnp.int32, (tq, pair_w), 0)
    col = lax.broadcasted_iota(jnp.int32, (tq, pair_w), 1)
    key_in_block = jnp.where(col >= blk, col - blk, col)
    head_dim = lax.broadcasted_iota(jnp.int32, (LANES, blk), 0)

    pairs = range(n_pairs)

    def key_block(j, carries, accs, diagonal):
        start = pl.multiple_of(j * blk, blk)
        carries, accs = list(carries), list(accs)
        if diagonal:
            causal = key_in_block + (j - i * kpq) * blk < row
        tri = tri_ref[...]
        zs = []
        for p in pairs:
            kTp = kT_ref[j, p * LANES:(p + 1) * LANES, :]
            zero = jnp.zeros_like(kTp)
            kbd = jnp.concatenate([jnp.where(head_dim < SB_HEAD_DIM, kTp, zero),
                                   jnp.where(head_dim >= SB_HEAD_DIM, kTp, zero)], axis=1)
            zs.append(_dot(q_ref[:, p * LANES:(p + 1) * LANES], kbd))
        costs = []
        for p in pairs:
            cost = jnp.maximum(zs[p], 0.0) + jnp.log2(1.0 + jnp.exp2(-jnp.abs(zs[p])))
            costs.append(jnp.where(causal, cost, 0.0) if diagonal else cost)
        rs = [_dot(costs[p].astype(BF16), tri) for p in pairs]
        for p in pairs:
            w = jnp.exp2(zs[p] - (rs[p][:, :pair_w] + carries[p]))
            if diagonal:
                w = jnp.where(causal, w, 0.0)
            vp = v_ref[pl.ds(start, blk), p * LANES:(p + 1) * LANES]
            zero = jnp.zeros_like(vp)
            vbd = jnp.concatenate([jnp.where(lane < SB_HEAD_DIM, vp, zero),
                                   jnp.where(lane >= SB_HEAD_DIM, vp, zero)], axis=0)
            accs[p] = accs[p] + _dot(w.astype(BF16), vbd)
            total = jnp.concatenate([jnp.broadcast_to(rs[p][:, a * blk:a * blk + 1], (tq, blk)) for a in range(2)],
                                    axis=1)
            carries[p] = carries[p] + total
        return carries, accs

    def save(carries, accs):
        for p in pairs:
            carry_scr[p] = carries[p]
            acc_scr[p] = accs[p]

    carries = [jnp.zeros((tq, pair_w), F32)] * n_pairs
    accs = [jnp.zeros((tq, LANES), F32)] * n_pairs
    for d in reversed(range(kpq)):
        carries, accs = key_block(i * kpq + d, carries, accs, True)
    save(carries, accs)

    def smallest_carry(carries):
        m = carries[0]
        for p in range(1, n_pairs):
            m = jnp.minimum(m, carries[p])
        return jnp.min(m)

    stop_at = stop_ref[0]

    def more_blocks(state):
        t, smallest = state
        return jnp.logical_and(t < i, smallest < stop_at)

    def body(state):
        t, _ = state
        carries = [carry_scr[p] for p in pairs]
        accs = [acc_scr[p] for p in pairs]
        for u in range(kpq):
            carries, accs = key_block((i - t) * kpq - 1 - u, carries, accs, False)
        save(carries, accs)
        return t + 1, smallest_carry(carries)

    lax.while_loop(more_blocks, body, (jnp.int32(0), smallest_carry(carries)))

    for p in pairs:
        o_ref[:, p * LANES:(p + 1) * LANES] = acc_scr[p].astype(o_ref.dtype)


def _sb_attention(stop_at, proj, kT, tri, batch, seq, blk, tq):
    t = proj.shape[0]
    nq = seq // tq
    nk = seq // blk
    width = SB_HEADS * SB_HEAD_DIM
    n_pairs = SB_HEADS // 2
    return pl.pallas_call(
        functools.partial(_sb_kernel, blk=blk),
        grid=(batch, nq),
        in_specs=[
            pl.BlockSpec(memory_space=pltpu.SMEM),
            pl.BlockSpec((tq, width), lambda b, i: (b * nq + i, CB_Q)),
            pl.BlockSpec((nk, width, blk), lambda b, i: (b, 0, 0)),
            pl.BlockSpec((seq, width), lambda b, i: (b, CB_V)),
            pl.BlockSpec(tri.shape, lambda b, i: (0, 0)),
        ],
        out_specs=pl.BlockSpec((tq, width), lambda b, i: (b * nq + i, 0)),
        out_shape=jax.ShapeDtypeStruct((t, width), BF16),
        scratch_shapes=[
            pltpu.VMEM((n_pairs, tq, 2 * blk), F32),
            pltpu.VMEM((n_pairs, tq, LANES), F32),
        ],
        compiler_params=pltpu.CompilerParams(
            dimension_semantics=("arbitrary", "arbitrary"), vmem_limit_bytes=VMEM_LIMIT),
        name="sbattn",
    )(stop_at, proj, kT, proj, tri)


def _mix_kernel(x_ref, osb_ref, cb_ref, cc_ref, cx_ref, cch_ref, cxh_ref, qm_ref,
                g0a_ref, g0b_ref, g1a_ref, g1b_ref, g2a_ref, g2b_ref,
                kmT_ref, vm_ref, convw_ref, wsb_ref, wcv_ref, wmm_ref, wo_ref,
                fg_ref, wrh_ref, wrl_ref, br_ref, ltri_ref,
                x1_ref, hf_ref, idx_ref, gate_ref, rank_ref, cnt_ref,
                run_scr, *, tm, tiles_per_seq):
    i = pl.program_id(0)

    @pl.when(i == 0)
    def _():
        run_scr[...] = jnp.zeros_like(run_scr)

    u = cc_ref[...].astype(F32) * cx_ref[...].astype(F32)
    halo = cch_ref[...].astype(F32) * cxh_ref[...].astype(F32)
    halo = jnp.where(i % tiles_per_seq == 0, 0.0, halo)
    prev1 = halo[-1:, :]
    prev2 = halo[-2:-1, :]
    rows = lax.broadcasted_iota(jnp.int32, u.shape, 0)
    u1 = jnp.where(rows == 0, prev1, pltpu.roll(u, 1, 0))
    u2 = jnp.where(rows == 0, prev2, jnp.where(rows == 1, prev1, pltpu.roll(u, 2, 0)))
    cw = convw_ref[...]
    o_conv = cb_ref[...].astype(F32) * (cw[0:1, :] * u2 + cw[1:2, :] * u1 + cw[2:3, :] * u)

    a_mem = None
    for h in range(MEM_HEADS):
        sl = slice(h * MEM_HEAD_DIM, (h + 1) * MEM_HEAD_DIM)
        s = _dot(qm_ref[:, sl], kmT_ref[0, sl, :])
        s = s - jnp.max(s, axis=-1, keepdims=True)
        e = jnp.exp(s)
        oh = _dot(e.astype(BF16), vm_ref[:, sl]) / jnp.sum(e, axis=-1, keepdims=True)
        part = _dot(oh.astype(BF16), wmm_ref[sl, :])
        a_mem = part if a_mem is None else a_mem + part

    a_sb = _dot(osb_ref[...], wsb_ref[...])
    a_cv = _dot(o_conv.astype(BF16), wcv_ref[...])

    half = a_sb.shape[1] // 2
    gates = ((g0a_ref, g1a_ref, g2a_ref), (g0b_ref, g1b_ref, g2b_ref))
    x1 = x_ref[...]
    for c in range(2):
        cs = slice(c * half, (c + 1) * half)
        g0, g1, g2 = (0.5 * jnp.tanh(0.5 * r[...].astype(F32)) + 0.5 for r in gates[c])
        merged = g0 * a_sb[:, cs] + g1 * a_cv[:, cs] + g2 * a_mem[:, cs]
        x1 = x1 + _dot(merged.astype(BF16), wo_ref[cs, :])
    x1_ref[...] = x1

    ms = jnp.mean(x1 * x1, axis=-1, keepdims=True)
    hf = x1 * lax.rsqrt(ms + RMS_EPS) * fg_ref[...]
    hf_ref[...] = hf
    hf_hi, hf_lo = _split_bf16(hf)
    logits = (_dot(hf_hi, wrh_ref[...]) + _dot(hf_lo, wrh_ref[...]) + _dot(hf_hi, wrl_ref[...])
              + br_ref[...])
    lane = lax.broadcasted_iota(jnp.int32, logits.shape, 1)
    vals, idxs = [], []
    l = logits
    for _ in range(TOP_K):
        m = jnp.max(l, axis=-1, keepdims=True)
        ik = jnp.min(jnp.where(l == m, lane, LANES), axis=-1, keepdims=True)
        vals.append(m)
        idxs.append(ik)
        l = jnp.where(lane == ik, -3.0e38, l)
    exps = [jnp.exp(v - vals[0]) for v in vals]
    denom = exps[0] + exps[1] + exps[2] + exps[3]

    sel = jnp.zeros(logits.shape, F32)
    for ik in idxs:
        sel = sel + jnp.where(lane == ik, 1.0, 0.0)
    rank_full = _dot(ltri_ref[...], sel.astype(BF16)) + run_scr[...]
    run_scr[...] = run_scr[...] + jnp.sum(sel, axis=0, keepdims=True)
    cnt_ref[...] = jnp.broadcast_to(run_scr[...], cnt_ref.shape)

    idx_out = jnp.zeros(logits.shape, jnp.int32)
    gate_out = jnp.zeros(logits.shape, F32)
    rank_out = jnp.zeros(logits.shape, F32)
    for k in range(TOP_K):
        rk = jnp.sum(jnp.where(lane == idxs[k], rank_full, 0.0), axis=-1, keepdims=True)
        idx_out = jnp.where(lane == k, idxs[k], idx_out)
        gate_out = jnp.where(lane == k, exps[k] / denom, gate_out)
        rank_out = jnp.where(lane == k, rk, rank_out)
    idx_ref[...] = idx_out.T[:idx_ref.shape[0]]
    gate_ref[...] = gate_out
    rank_ref[...] = rank_out.astype(jnp.int32).T[:rank_ref.shape[0]]


def _mixer(x2, osb, proj, kmT, vmem, conv_w, wsb, wcv, wmm, wo, fg, wr_hi, wr_lo, br_pad, ltri,
           seq, mem_tokens, tm):
    t, d = x2.shape
    width = COL_BLOCK
    tiles_per_seq = seq // tm
    halo_rows = 16
    hb = tm // halo_rows
    gate_blocks_per_branch = d // COL_BLOCK

    def colspec(cb):
        return pl.BlockSpec((tm, width), lambda i: (i, cb))

    def halospec(cb):
        return pl.BlockSpec((halo_rows, width), lambda i: (jnp.maximum(i * hb - 1, 0), cb))

    def full(a):
        nd = a.ndim
        return pl.BlockSpec(a.shape, lambda i: (0,) * nd)

    gate_specs = [colspec(CB_GATES + br * gate_blocks_per_branch + c)
                  for br in range(3) for c in range(gate_blocks_per_branch)]
    in_specs = [
        pl.BlockSpec((tm, d), lambda i: (i, 0)),
        pl.BlockSpec((tm, width), lambda i: (i, 0)),
        colspec(CB_CONV_B), colspec(CB_CONV_C), colspec(CB_CONV_X),
        halospec(CB_CONV_C), halospec(CB_CONV_X),
        colspec(CB_MEM_Q),
        *gate_specs,
        pl.BlockSpec((1, width, mem_tokens), lambda i: (i // tiles_per_seq, 0, 0)),
        pl.BlockSpec((mem_tokens, width), lambda i: (i // tiles_per_seq, 0)),
        full(conv_w), full(wsb), full(wcv), full(wmm), full(wo), full(fg),
        full(wr_hi), full(wr_lo), full(br_pad), full(ltri),
    ]
    out_shape = (
        jax.ShapeDtypeStruct((t, d), F32),
        jax.ShapeDtypeStruct((t, d), F32),
        jax.ShapeDtypeStruct((8, t), jnp.int32),
        jax.ShapeDtypeStruct((t, LANES), F32),
        jax.ShapeDtypeStruct((8, t), jnp.int32),
        jax.ShapeDtypeStruct((8, LANES), F32),
    )
    out_specs = (
        pl.BlockSpec((tm, d), lambda i: (i, 0)),
        pl.BlockSpec((tm, d), lambda i: (i, 0)),
        pl.BlockSpec((8, tm), lambda i: (0, i)),
        pl.BlockSpec((tm, LANES), lambda i: (i, 0)),
        pl.BlockSpec((8, tm), lambda i: (0, i)),
        pl.BlockSpec((8, LANES), lambda i: (0, 0)),
    )
    n_proj_views = 6 + 3 * gate_blocks_per_branch
    return pl.pallas_call(
        functools.partial(_mix_kernel, tm=tm, tiles_per_seq=tiles_per_seq),
        grid=(t // tm,),
        in_specs=in_specs,
        out_specs=out_specs,
        out_shape=out_shape,
        scratch_shapes=[pltpu.VMEM((1, LANES), F32)],
        compiler_params=pltpu.CompilerParams(
            dimension_semantics=("arbitrary",), vmem_limit_bytes=VMEM_LIMIT),
        name="mixer",
    )(x2, osb, *([proj] * n_proj_views), kmT, vmem, conv_w, wsb, wcv, wmm, wo, fg, wr_hi, wr_lo, br_pad, ltri)


DMA_UNROLL = 8


def _start_row_copies(n, make_copy):
    def body(g, c):
        for u in range(DMA_UNROLL):
            make_copy(g * DMA_UNROLL + u).start(priority=u % 2)
        return c

    lax.fori_loop(0, n // DMA_UNROLL, body, 0)


def _dispatch_kernel(dest_ref, pad_ref, hf_ref, xs_hbm, rows_scr, zrow, sems):
    i = pl.program_id(0)
    tm = hf_ref.shape[0]
    n_pad = pad_ref.shape[0]

    @pl.when(i == 0)
    def _():
        zrow[...] = jnp.zeros_like(zrow)
        _start_row_copies(n_pad, lambda j: pltpu.make_async_copy(zrow.at[0], xs_hbm.at[pad_ref[j]], sems.at[1]))

    rows_scr[...] = hf_ref[...].reshape(rows_scr.shape)
    n_tok = pl.num_programs(0) * tm
    first = i * tm

    def token_group(g, c):
        for u in range(DMA_UNROLL // TOP_K):
            r = g * (DMA_UNROLL // TOP_K) + u
            for k in range(TOP_K):
                pltpu.make_async_copy(rows_scr.at[r], xs_hbm.at[dest_ref[k * n_tok + first + r]],
                                      sems.at[0]).start(priority=k % 2)
        return c

    lax.fori_loop(0, tm * TOP_K // DMA_UNROLL, token_group, 0)
    for _ in range(TOP_K):
        pltpu.make_async_copy(rows_scr, xs_hbm.at[pl.ds(0, tm)], sems.at[0]).wait()

    @pl.when(i == pl.num_programs(0) - 1)
    def _():
        rows = xs_hbm.at[pl.ds(0, n_pad)]
        pltpu.make_async_copy(rows, rows, sems.at[1]).wait()


def _dispatch(dest_flat, pad_dest, hf, n_rows_total, tm):
    t, d = hf.shape
    grid_spec = pltpu.PrefetchScalarGridSpec(
        num_scalar_prefetch=2,
        grid=(t // tm,),
        in_specs=[pl.BlockSpec((tm, d), lambda i, dr, pr: (i, 0))],
        out_specs=pl.BlockSpec(memory_space=pl.ANY),
        scratch_shapes=[pltpu.VMEM((tm, 1, d), F32), pltpu.VMEM((8, 1, d), F32),
                        pltpu.SemaphoreType.DMA((2,))],
    )
    return pl.pallas_call(
        _dispatch_kernel,
        grid_spec=grid_spec,
        out_shape=jax.ShapeDtypeStruct((n_rows_total, 1, d), F32),
        compiler_params=pltpu.CompilerParams(
            dimension_semantics=("arbitrary",), vmem_limit_bytes=VMEM_LIMIT),
        name="dispatch",
    )(dest_flat, pad_dest, hf)


def _expert_kernel(be_ref, nu_ref, slot_ref, nxt_ref, x_ref, wgu_hbm, bgu_ref, wdn_hbm, bdn_ref, y_ref,
                   x2d, wgu_f32, wdn_f32, wgu_bf, wdn_bf, sems):
    i = pl.program_id(0)
    n_used = nu_ref[0]

    def weight_copies(e, slot):
        return (pltpu.make_async_copy(wgu_hbm.at[e], wgu_f32.at[slot], sems.at[0, slot]),
                pltpu.make_async_copy(wdn_hbm.at[e], wdn_f32.at[slot], sems.at[1, slot]))

    first_of_expert = jnp.logical_or(i == 0, be_ref[i] != be_ref[jnp.maximum(i - 1, 0)])
    for slot in range(2):
        @pl.when(jnp.logical_and(jnp.logical_and(first_of_expert, i < n_used), slot_ref[i] == slot))
        def _():
            @pl.when(i == 0)
            def _():
                for cp in weight_copies(be_ref[0], slot):
                    cp.start()

            for cp in weight_copies(be_ref[i], slot):
                cp.wait()
            wgu_bf[...] = wgu_f32[slot].astype(BF16)
            wdn_bf[...] = wdn_f32[slot].astype(BF16)

            @pl.when(nxt_ref[i] >= 0)
            def _():
                for cp in weight_copies(nxt_ref[i], 1 - slot):
                    cp.start()

    @pl.when(i < n_used)
    def _():
        x2d[...] = x_ref[...].reshape(x2d.shape)
        gu = _dot(x2d[...].astype(BF16), wgu_bf[...]) + bgu_ref[0]
        f = gu.shape[1] // 2
        g = jnp.minimum(gu[:, :f], SWIGLU_LIMIT)
        lin = jnp.clip(gu[:, f:], -SWIGLU_LIMIT, SWIGLU_LIMIT)
        act = g * (0.5 * jnp.tanh((0.5 * SWIGLU_ALPHA) * g) + 0.5) * (lin + 1.0)
        y = _dot(act.astype(BF16), wdn_bf[...]) + bdn_ref[0]
        y_ref[...] = y.reshape(y_ref.shape)

    @pl.when(i >= n_used)
    def _():
        y_ref[...] = jnp.zeros_like(y_ref)


def _experts(block_e, n_used, w_slot, next_e, xs3, n_rows, wgu, bgu, wdn, bdn, tmoe):
    d = xs3.shape[-1]
    e, _, f2 = wgu.shape

    def x_map(i, be, nu, ws, ne):
        return (jnp.minimum(i, jnp.maximum(nu[0] - 1, 0)), 0, 0)

    grid_spec = pltpu.PrefetchScalarGridSpec(
        num_scalar_prefetch=4,
        grid=(n_rows // tmoe,),
        in_specs=[
            pl.BlockSpec((tmoe, 1, d), x_map),
            pl.BlockSpec(memory_space=pl.ANY),
            pl.BlockSpec((1, 1, f2), lambda i, be, nu, ws, ne: (be[i], 0, 0)),
            pl.BlockSpec(memory_space=pl.ANY),
            pl.BlockSpec((1, 1, d), lambda i, be, nu, ws, ne: (be[i], 0, 0)),
        ],
        out_specs=pl.BlockSpec((tmoe, 1, d), lambda i, be, nu, ws, ne: (i, 0, 0)),
        scratch_shapes=[
            pltpu.VMEM((tmoe, d), F32),
            pltpu.VMEM((2, d, f2), F32),
            pltpu.VMEM((2, f2 // 2, d), F32),
            pltpu.VMEM((d, f2), BF16),
            pltpu.VMEM((f2 // 2, d), BF16),
            pltpu.SemaphoreType.DMA((2, 2)),
        ],
    )
    return pl.pallas_call(
        _expert_kernel,
        grid_spec=grid_spec,
        out_shape=jax.ShapeDtypeStruct((n_rows, 1, d), F32),
        compiler_params=pltpu.CompilerParams(
            dimension_semantics=("arbitrary",), vmem_limit_bytes=VMEM_LIMIT),
        name="experts",
    )(block_e, n_used, w_slot, next_e, xs3, wgu, bgu, wdn, bdn)


def _combine_kernel(dest_ref, x1_ref, gate_ref, y_hbm, o_ref, buf0, buf1, y2d, sems):
    i = pl.program_id(0)
    n = pl.num_programs(0)
    rows, _, d = buf0.shape
    tm = rows // TOP_K
    bufs = (buf0, buf1)

    def start_gather(tile, slot):
        first = tile * rows
        _start_row_copies(rows, lambda r: pltpu.make_async_copy(
            y_hbm.at[dest_ref[first + r]], bufs[slot].at[r], sems.at[slot]))

    @pl.when(i == 0)
    def _():
        start_gather(0, 0)

    for par in range(2):
        @pl.when(i % 2 == par)
        def _():
            @pl.when(i + 1 < n)
            def _():
                start_gather(i + 1, 1 - par)

            pltpu.make_async_copy(y_hbm.at[pl.ds(0, rows)], bufs[par], sems.at[par]).wait()
            y2d[...] = bufs[par][...].reshape(rows, d)

    acc = x1_ref[...]
    gate = gate_ref[...]
    for k in range(TOP_K):
        acc = acc + gate[:, k:k + 1] * y2d[k * tm:(k + 1) * tm, :]
    o_ref[...] = acc


def _combine(dest_tiled, x1, gate, y3, tm):
    t, d = x1.shape
    rows = tm * TOP_K
    grid_spec = pltpu.PrefetchScalarGridSpec(
        num_scalar_prefetch=1,
        grid=(t // tm,),
        in_specs=[
            pl.BlockSpec((tm, d), lambda i, dr: (i, 0)),
            pl.BlockSpec((tm, LANES), lambda i, dr: (i, 0)),
            pl.BlockSpec(memory_space=pl.ANY),
        ],
        out_specs=pl.BlockSpec((tm, d), lambda i, dr: (i, 0)),
        scratch_shapes=[
            pltpu.VMEM((rows, 1, d), F32),
            pltpu.VMEM((rows, 1, d), F32),
            pltpu.VMEM((rows, d), F32),
            pltpu.SemaphoreType.DMA((2,)),
        ],
    )
    return pl.pallas_call(
        _combine_kernel,
        grid_spec=grid_spec,
        out_shape=jax.ShapeDtypeStruct((t, d), F32),
        compiler_params=pltpu.CompilerParams(
            dimension_semantics=("arbitrary",), vmem_limit_bytes=VMEM_LIMIT),
        name="combine",
    )(dest_tiled, x1, gate, y3)


def _layer(x, mem, mix_norm_g, w_in, sb_q_norm_g, sb_k_norm_g, conv_w, mem_norm_g, w_mem_kv,
           mem_q_norm_g, mem_k_norm_g, w_br_sb, w_br_conv, w_br_mem, w_o, ffn_norm_g,
           w_router, b_router, w_gate_up, b_gate_up, w_down, b_down):
    b, s, d = x.shape
    m = mem.shape[1]
    t = b * s
    n_in = w_in.shape[1]
    sb_blk = 128
    sb_tq = 256
    tm_proj = 512
    tm_mix = 512
    tmoe = 256
    tm_comb = 256
    tm_disp = 512

    x2 = x.reshape(t, d)
    nm64 = _group_sum_matrix(COL_BLOCK, SB_HEAD_DIM)
    nm128 = _group_sum_matrix(COL_BLOCK, MEM_HEAD_DIM)

    col_gain = jnp.ones((n_in,), F32)
    col_gain = col_gain.at[CB_Q * COL_BLOCK:(CB_Q + 1) * COL_BLOCK].set(
        jnp.tile(sb_q_norm_g, SB_HEADS) * (SB_HEAD_DIM ** -0.5 * LOG2_E))
    col_gain = col_gain.at[CB_K * COL_BLOCK:(CB_K + 1) * COL_BLOCK].set(jnp.tile(sb_k_norm_g, SB_HEADS))
    col_gain = col_gain.at[CB_MEM_Q * COL_BLOCK:(CB_MEM_Q + 1) * COL_BLOCK].set(
        jnp.tile(mem_q_norm_g, MEM_HEADS) * (MEM_HEAD_DIM ** -0.5))

    proj, kT = _inproj(x2, mix_norm_g.reshape(1, d), w_in.astype(BF16),
                       jnp.stack([nm64, nm128]), col_gain.reshape(1, n_in), tm_proj)

    km, vm = _memkv(mem.reshape(b * m, d), mem_norm_g.reshape(1, d), w_mem_kv.astype(BF16), nm128,
                    jnp.tile(mem_k_norm_g, MEM_HEADS).reshape(1, -1))
    kmT = km.reshape(b, m, -1).transpose(0, 2, 1)

    jj = np.arange(sb_blk)
    later = (jj[:, None] >= jj[None, :]).astype(np.float32)
    ones = np.ones((sb_blk, sb_blk), np.float32)
    eye = np.eye(2, dtype=np.float32)
    tri = np.kron(eye, later)
    z_bound = (SB_HEAD_DIM * (SB_HEAD_DIM ** -0.5 * LOG2_E) * 1.02) * jnp.max(jnp.abs(sb_q_norm_g * sb_k_norm_g))
    stop_at = (151.0 + z_bound).reshape(1).astype(F32)
    o_sb = _sb_attention(stop_at, proj, kT, jnp.asarray(tri, dtype=BF16), b, s, sb_blk, sb_tq)

    wr_pad = jnp.zeros((d, LANES), F32).at[:, :N_EXPERTS].set(w_router)
    wr_hi, wr_lo = _split_bf16(wr_pad)
    br_pad = jnp.full((1, LANES), -1.0e30, F32).at[0, :N_EXPERTS].set(b_router)
    rr = np.arange(tm_mix)
    ltri = jnp.asarray(rr[None, :] < rr[:, None], dtype=BF16)
    x1, hf, idx, gate, rank, cnt = _mixer(
        x2, o_sb, proj, kmT, vm, conv_w, w_br_sb.astype(BF16), w_br_conv.astype(BF16),
        w_br_mem.astype(BF16), w_o.astype(BF16), ffn_norm_g.reshape(1, d), wr_hi, wr_lo, br_pad, ltri,
        s, m, tm_mix)

    a = t * TOP_K
    counts = cnt[0, :N_EXPERTS].astype(jnp.int32)
    padded = (counts + tmoe - 1) // tmoe * tmoe
    pend = jnp.cumsum(padded)
    pstart = pend - padded
    experts_col = jnp.arange(N_EXPERTS, dtype=jnp.int32)[:, None, None]
    dest = rank[:TOP_K] + jnp.sum(jnp.where(idx[None, :TOP_K] == experts_col, pstart[:, None, None], 0), axis=0)
    n_blocks = a // tmoe + N_EXPERTS
    n_rows = n_blocks * tmoe
    block_row0 = jnp.arange(n_blocks, dtype=jnp.int32) * tmoe
    block_e = jnp.minimum(jnp.sum((pend[None, :] <= block_row0[:, None]).astype(jnp.int32), axis=1),
                          N_EXPERTS - 1)
    n_used = (pend[-1] // tmoe).astype(jnp.int32).reshape(1)
    slot = jnp.arange(tmoe, dtype=jnp.int32)
    n_pad = padded - counts
    is_pad = slot[None, :] < n_pad[:, None]
    spare_index = (jnp.arange(N_EXPERTS, dtype=jnp.int32) * tmoe - jnp.cumsum(n_pad))[:, None] + slot[None, :]
    pad_dest = jnp.where(is_pad, (pstart + counts)[:, None] + slot[None, :], pend[-1] + spare_index).reshape(-1)
    xs3 = _dispatch(dest.reshape(a), pad_dest, hf, n_rows, tm_disp)
    has_rows = counts > 0
    w_slot = ((jnp.cumsum(has_rows.astype(jnp.int32)) - 1) % 2)[block_e]
    later_e = jnp.where(has_rows[None, :] & (jnp.arange(N_EXPERTS)[None, :] > jnp.arange(N_EXPERTS)[:, None]),
                        jnp.arange(N_EXPERTS, dtype=jnp.int32)[None, :], N_EXPERTS)
    next_e = jnp.min(later_e, axis=1)
    next_e = jnp.where(next_e < N_EXPERTS, next_e, -1)[block_e]
    y3 = _experts(block_e, n_used, w_slot.astype(jnp.int32), next_e.astype(jnp.int32), xs3, n_rows,
                  w_gate_up, b_gate_up[:, None, :], w_down, b_down[:, None, :], tmoe)
    dest_tiled = dest.reshape(TOP_K, t // tm_comb, tm_comb).transpose(1, 0, 2).reshape(a)
    out = _combine(dest_tiled, x1, gate, y3, tm_comb)
    return out.reshape(b, s, d)


def kernel(x, mem, mix_norm_g, w_in, sb_q_norm_g, sb_k_norm_g, conv_w, mem_norm_g, w_mem_kv,
           mem_q_norm_g, mem_k_norm_g, w_br_sb, w_br_conv, w_br_mem, w_o, ffn_norm_g,
           w_router, b_router, w_gate_up, b_gate_up, w_down, b_down):
    depth = mix_norm_g.shape[0]
    for l in range(depth):
        x = _layer(x, mem, mix_norm_g[l], w_in[l], sb_q_norm_g[l], sb_k_norm_g[l], conv_w[l],
                   mem_norm_g[l], w_mem_kv[l], mem_q_norm_g[l], mem_k_norm_g[l], w_br_sb[l],
                   w_br_conv[l], w_br_mem[l], w_o[l], ffn_norm_g[l], w_router[l], b_router[l],
                   w_gate_up[l], b_gate_up[l], w_down[l], b_down[l])
    return x
```

```python
import functools

import numpy as np
import jax
import jax.numpy as jnp
from jax import lax
from jax.experimental import pallas as pl
from jax.experimental.pallas import tpu as pltpu

F32 = jnp.float32
BF16 = jnp.bfloat16

RMS_EPS = 1e-6
SB_HEADS = 8
SB_HEAD_DIM = 64
MEM_HEADS = 4
MEM_HEAD_DIM = 128
N_EXPERTS = 32
TOP_K = 4
SWIGLU_LIMIT = 7.0
SWIGLU_ALPHA = 1.702
LOG2_E = 1.4426950408889634
F32_UNDERFLOW_LOG2 = 151.0
BF16_SLACK = 1.02
PAD_LOGIT = -1.0e30
TAKEN_LOGIT = -3.0e38

LANES = 128
COL_BLOCK = 512
CB_Q, CB_K, CB_V, CB_CONV_B, CB_CONV_C, CB_CONV_X, CB_MEM_Q, CB_GATES = range(8)

VMEM_LIMIT = 56 * 1024 * 1024


def _dot(a, b):
    return jnp.dot(a, b, preferred_element_type=F32)


def _split_bf16(x):
    hi = x.astype(BF16)
    lo = (x - hi.astype(F32)).astype(BF16)
    return hi, lo


def _group_sum_matrix(width, group):
    idx = np.arange(width) // group
    return jnp.asarray(idx[:, None] == idx[None, :], dtype=BF16)


def _memkv_kernel(mem_ref, g_ref, w_ref, nm_ref, kg_ref, k_ref, v_ref):
    xf = mem_ref[...]
    ms = jnp.mean(xf * xf, axis=-1, keepdims=True)
    h = (xf * lax.rsqrt(ms + RMS_EPS) * g_ref[...]).astype(BF16)
    kv = _dot(h, w_ref[...])
    width = k_ref.shape[-1]
    k = kv[:, :width]
    hi, lo = _split_bf16(k * k)
    ss = _dot(hi, nm_ref[...]) + _dot(lo, nm_ref[...])
    k_ref[...] = (k * lax.rsqrt(ss * (1.0 / MEM_HEAD_DIM) + RMS_EPS) * kg_ref[...]).astype(BF16)
    v_ref[...] = kv[:, width:].astype(BF16)


def _memkv(mem2, g, w_bf, nm128, kg_cols):
    rows, d = mem2.shape
    width = w_bf.shape[1] // 2
    return pl.pallas_call(
        _memkv_kernel,
        out_shape=(jax.ShapeDtypeStruct((rows, width), BF16),
                   jax.ShapeDtypeStruct((rows, width), BF16)),
        compiler_params=pltpu.CompilerParams(vmem_limit_bytes=VMEM_LIMIT),
        name="memkv",
    )(mem2, g, w_bf, nm128, kg_cols)


def _inproj_kernel(x_ref, g_ref, w_ref, nm_ref, cg_ref, o_ref, kT_ref):
    xf = x_ref[...]
    ms = jnp.mean(xf * xf, axis=-1, keepdims=True)
    h = (xf * lax.rsqrt(ms + RMS_EPS) * g_ref[...]).astype(BF16)
    for j in range(w_ref.shape[1] // COL_BLOCK):
        cs = slice(j * COL_BLOCK, (j + 1) * COL_BLOCK)
        p = _dot(h, w_ref[:, cs])
        if j in (CB_Q, CB_K, CB_MEM_Q):
            group, nm = (MEM_HEAD_DIM, nm_ref[1]) if j == CB_MEM_Q else (SB_HEAD_DIM, nm_ref[0])
            ss = _dot((p * p).astype(BF16), nm)
            p = p * lax.rsqrt(ss * (1.0 / group) + RMS_EPS) * cg_ref[:, cs]
        if j == CB_K:
            for c in range(kT_ref.shape[0]):
                kT_ref[c] = p[c * LANES:(c + 1) * LANES, :].T.astype(BF16)
        o_ref[:, cs] = p.astype(BF16)


def _inproj(x2, g, w_bf, nmats, col_gain, tm):
    t, d = x2.shape
    n = w_bf.shape[1]

    def resident(a):
        nd = a.ndim
        return pl.BlockSpec(a.shape, lambda i: (0,) * nd)

    return pl.pallas_call(
        _inproj_kernel,
        grid=(t // tm,),
        in_specs=[pl.BlockSpec((tm, d), lambda i: (i, 0)),
                  resident(g), resident(w_bf), resident(nmats), resident(col_gain)],
        out_specs=(pl.BlockSpec((tm, n), lambda i: (i, 0)),
                   pl.BlockSpec((tm // LANES, COL_BLOCK, LANES), lambda i: (i, 0, 0))),
        out_shape=(jax.ShapeDtypeStruct((t, n), BF16),
                   jax.ShapeDtypeStruct((t // LANES, COL_BLOCK, LANES), BF16)),
        compiler_params=pltpu.CompilerParams(
            dimension_semantics=("arbitrary",), vmem_limit_bytes=VMEM_LIMIT),
        name="inproj",
    )(x2, g, w_bf, nmats, col_gain)


def _sb_kernel(stop_ref, q_ref, kT_ref, v_ref, tri_ref, o_ref, carry_scr, acc_scr, *, blk):
    i = pl.program_id(1)
    n_pairs = SB_HEADS // 2
    pair_w = 2 * blk
    tq = q_ref.shape[0]
    kpq = tq // blk
    lane = lax.broadcasted_iota(jnp.int32, (1, LANES), 1)
    row = lax.broadcasted_iota(jnp.int32, (tq, pair_w), 0)
    col = lax.broadcasted_iota(jnp.int32, (tq, pair_w), 1)
    key_in_block = jnp.where(col >= blk, col - blk, col)
    head_dim = lax.broadcasted_iota(jnp.int32, (LANES, blk), 0)

    pairs = range(n_pairs)

    def key_block(j, carries, accs, diagonal):
        start = pl.multiple_of(j * blk, blk)
        carries, accs = list(carries), list(accs)
        if diagonal:
            causal = key_in_block + (j - i * kpq) * blk < row
        tri = tri_ref[...]
        zs = []
        for p in pairs:
            kTp = kT_ref[j, p * LANES:(p + 1) * LANES, :]
            zero = jnp.zeros_like(kTp)
            kbd = jnp.concatenate([jnp.where(head_dim < SB_HEAD_DIM, kTp, zero),
                                   jnp.where(head_dim >= SB_HEAD_DIM, kTp, zero)], axis=1)
            zs.append(_dot(q_ref[:, p * LANES:(p + 1) * LANES], kbd))
        costs = []
        for p in pairs:
            cost = jnp.maximum(zs[p], 0.0) + jnp.log2(1.0 + jnp.exp2(-jnp.abs(zs[p])))
            costs.append(jnp.where(causal, cost, 0.0) if diagonal else cost)
        rs = [_dot(costs[p].astype(BF16), tri) for p in pairs]
        for p in pairs:
            w = jnp.exp2(zs[p] - (rs[p][:, :pair_w] + carries[p]))
            if diagonal:
                w = jnp.where(causal, w, 0.0)
            vp = v_ref[pl.ds(start, blk), p * LANES:(p + 1) * LANES]
            zero = jnp.zeros_like(vp)
            vbd = jnp.concatenate([jnp.where(lane < SB_HEAD_DIM, vp, zero),
                                   jnp.where(lane >= SB_HEAD_DIM, vp, zero)], axis=0)
            accs[p] = accs[p] + _dot(w.astype(BF16), vbd)
            total = jnp.concatenate([jnp.broadcast_to(rs[p][:, a * blk:a * blk + 1], (tq, blk)) for a in range(2)],
                                    axis=1)
            carries[p] = carries[p] + total
        return carries, accs

    def save(carries, accs):
        for p in pairs:
            carry_scr[p] = carries[p]
            acc_scr[p] = accs[p]

    carries = [jnp.zeros((tq, pair_w), F32)] * n_pairs
    accs = [jnp.zeros((tq, LANES), F32)] * n_pairs
    for d in reversed(range(kpq)):
        carries, accs = key_block(i * kpq + d, carries, accs, True)
    save(carries, accs)

    def smallest_carry(carries):
        m = carries[0]
        for p in range(1, n_pairs):
            m = jnp.minimum(m, carries[p])
        return jnp.min(m)

    stop_at = stop_ref[0]

    def more_blocks(state):
        t, smallest = state
        return jnp.logical_and(t < i, smallest < stop_at)

    def body(state):
        t, _ = state
        carries = [carry_scr[p] for p in pairs]
        accs = [acc_scr[p] for p in pairs]
        for u in range(kpq):
            carries, accs = key_block((i - t) * kpq - 1 - u, carries, accs, False)
        save(carries, accs)
        return t + 1, smallest_carry(carries)

    lax.while_loop(more_blocks, body, (jnp.int32(0), smallest_carry(carries)))

    for p in pairs:
        o_ref[:, p * LANES:(p + 1) * LANES] = acc_scr[p].astype(o_ref.dtype)


def _sb_attention(stop_at, proj, kT, tri, batch, seq, blk, tq):
    t = proj.shape[0]
    nq = seq // tq
    nk = seq // blk
    width = SB_HEADS * SB_HEAD_DIM
    n_pairs = SB_HEADS // 2
    return pl.pallas_call(
        functools.partial(_sb_kernel, blk=blk),
        grid=(batch, nq),
        in_specs=[
            pl.BlockSpec(memory_space=pltpu.SMEM),
            pl.BlockSpec((tq, width), lambda b, i: (b * nq + i, CB_Q)),
            pl.BlockSpec((nk, width, blk), lambda b, i: (b, 0, 0)),
            pl.BlockSpec((seq, width), lambda b, i: (b, CB_V)),
            pl.BlockSpec(tri.shape, lambda b, i: (0, 0)),
        ],
        out_specs=pl.BlockSpec((tq, width), lambda b, i: (b * nq + i, 0)),
        out_shape=jax.ShapeDtypeStruct((t, width), BF16),
        scratch_shapes=[
            pltpu.VMEM((n_pairs, tq, 2 * blk), F32),
            pltpu.VMEM((n_pairs, tq, LANES), F32),
        ],
        compiler_params=pltpu.CompilerParams(
            dimension_semantics=("arbitrary", "arbitrary"), vmem_limit_bytes=VMEM_LIMIT),
        name="sbattn",
    )(stop_at, proj, kT, proj, tri)


def _mix_kernel(x_ref, osb_ref, cb_ref, cc_ref, cx_ref, cch_ref, cxh_ref, qm_ref,
                g0a_ref, g0b_ref, g1a_ref, g1b_ref, g2a_ref, g2b_ref,
                kmT_ref, vm_ref, convw_ref, wsb_ref, wcv_ref, wmm_ref, wo_ref,
                fg_ref, wrh_ref, wrl_ref, br_ref, ltri_ref,
                x1_ref, hf_ref, idx_ref, gate_ref, rank_ref, cnt_ref,
                run_scr, *, tm, tiles_per_seq):
    i = pl.program_id(0)

    @pl.when(i == 0)
    def _():
        run_scr[...] = jnp.zeros_like(run_scr)

    u = cc_ref[...].astype(F32) * cx_ref[...].astype(F32)
    halo = cch_ref[...].astype(F32) * cxh_ref[...].astype(F32)
    halo = jnp.where(i % tiles_per_seq == 0, 0.0, halo)
    prev1 = halo[-1:, :]
    prev2 = halo[-2:-1, :]
    rows = lax.broadcasted_iota(jnp.int32, u.shape, 0)
    u1 = jnp.where(rows == 0, prev1, pltpu.roll(u, 1, 0))
    u2 = jnp.where(rows == 0, prev2, jnp.where(rows == 1, prev1, pltpu.roll(u, 2, 0)))
    cw = convw_ref[...]
    o_conv = cb_ref[...].astype(F32) * (cw[0:1, :] * u2 + cw[1:2, :] * u1 + cw[2:3, :] * u)

    a_mem = None
    for h in range(MEM_HEADS):
        sl = slice(h * MEM_HEAD_DIM, (h + 1) * MEM_HEAD_DIM)
        s = _dot(qm_ref[:, sl], kmT_ref[0, sl, :])
        s = s - jnp.max(s, axis=-1, keepdims=True)
        e = jnp.exp(s)
        oh = _dot(e.astype(BF16), vm_ref[:, sl]) / jnp.sum(e, axis=-1, keepdims=True)
        part = _dot(oh.astype(BF16), wmm_ref[sl, :])
        a_mem = part if a_mem is None else a_mem + part

    a_sb = _dot(osb_ref[...], wsb_ref[...])
    a_cv = _dot(o_conv.astype(BF16), wcv_ref[...])

    half = a_sb.shape[1] // 2
    gates = ((g0a_ref, g1a_ref, g2a_ref), (g0b_ref, g1b_ref, g2b_ref))
    x1 = x_ref[...]
    for c in range(2):
        cs = slice(c * half, (c + 1) * half)
        g0, g1, g2 = (0.5 * jnp.tanh(0.5 * r[...].astype(F32)) + 0.5 for r in gates[c])
        merged = g0 * a_sb[:, cs] + g1 * a_cv[:, cs] + g2 * a_mem[:, cs]
        x1 = x1 + _dot(merged.astype(BF16), wo_ref[cs, :])
    x1_ref[...] = x1

    ms = jnp.mean(x1 * x1, axis=-1, keepdims=True)
    hf = x1 * lax.rsqrt(ms + RMS_EPS) * fg_ref[...]
    hf_ref[...] = hf
    hf_hi, hf_lo = _split_bf16(hf)
    logits = (_dot(hf_hi, wrh_ref[...]) + _dot(hf_lo, wrh_ref[...]) + _dot(hf_hi, wrl_ref[...])
              + br_ref[...])
    lane = lax.broadcasted_iota(jnp.int32, logits.shape, 1)
    vals, idxs = [], []
    l = logits
    for _ in range(TOP_K):
        m = jnp.max(l, axis=-1, keepdims=True)
        ik = jnp.min(jnp.where(l == m, lane, LANES), axis=-1, keepdims=True)
        vals.append(m)
        idxs.append(ik)
        l = jnp.where(lane == ik, TAKEN_LOGIT, l)
    exps = [jnp.exp(v - vals[0]) for v in vals]
    denom = exps[0] + exps[1] + exps[2] + exps[3]

    sel = jnp.zeros(logits.shape, F32)
    for ik in idxs:
        sel = sel + jnp.where(lane == ik, 1.0, 0.0)
    rank_full = _dot(ltri_ref[...], sel.astype(BF16)) + run_scr[...]
    run_scr[...] = run_scr[...] + jnp.sum(sel, axis=0, keepdims=True)
    cnt_ref[...] = jnp.broadcast_to(run_scr[...], cnt_ref.shape)

    idx_out = jnp.zeros(logits.shape, jnp.int32)
    gate_out = jnp.zeros(logits.shape, F32)
    rank_out = jnp.zeros(logits.shape, F32)
    for k in range(TOP_K):
        rk = jnp.sum(jnp.where(lane == idxs[k], rank_full, 0.0), axis=-1, keepdims=True)
        idx_out = jnp.where(lane == k, idxs[k], idx_out)
        gate_out = jnp.where(lane == k, exps[k] / denom, gate_out)
        rank_out = jnp.where(lane == k, rk, rank_out)
    idx_ref[...] = idx_out.T[:idx_ref.shape[0]]
    gate_ref[...] = gate_out
    rank_ref[...] = rank_out.astype(jnp.int32).T[:rank_ref.shape[0]]


def _mixer(x2, osb, proj, kmT, vmem, conv_w, wsb, wcv, wmm, wo, fg, wr_hi, wr_lo, br_pad, ltri,
           seq, mem_tokens, tm):
    t, d = x2.shape
    width = COL_BLOCK
    tiles_per_seq = seq // tm
    halo_rows = 16
    hb = tm // halo_rows
    gate_blocks_per_branch = d // COL_BLOCK

    def colspec(cb):
        return pl.BlockSpec((tm, width), lambda i: (i, cb))

    def halospec(cb):
        return pl.BlockSpec((halo_rows, width), lambda i: (jnp.maximum(i * hb - 1, 0), cb))

    def full(a):
        nd = a.ndim
        return pl.BlockSpec(a.shape, lambda i: (0,) * nd)

    gate_specs = [colspec(CB_GATES + br * gate_blocks_per_branch + c)
                  for br in range(3) for c in range(gate_blocks_per_branch)]
    in_specs = [
        pl.BlockSpec((tm, d), lambda i: (i, 0)),
        pl.BlockSpec((tm, width), lambda i: (i, 0)),
        colspec(CB_CONV_B), colspec(CB_CONV_C), colspec(CB_CONV_X),
        halospec(CB_CONV_C), halospec(CB_CONV_X),
        colspec(CB_MEM_Q),
        *gate_specs,
        pl.BlockSpec((1, width, mem_tokens), lambda i: (i // tiles_per_seq, 0, 0)),
        pl.BlockSpec((mem_tokens, width), lambda i: (i // tiles_per_seq, 0)),
        full(conv_w), full(wsb), full(wcv), full(wmm), full(wo), full(fg),
        full(wr_hi), full(wr_lo), full(br_pad), full(ltri),
    ]
    out_shape = (
        jax.ShapeDtypeStruct((t, d), F32),
        jax.ShapeDtypeStruct((t, d), F32),
        jax.ShapeDtypeStruct((8, t), jnp.int32),
        jax.ShapeDtypeStruct((t, LANES), F32),
        jax.ShapeDtypeStruct((8, t), jnp.int32),
        jax.ShapeDtypeStruct((8, LANES), F32),
    )
    out_specs = (
        pl.BlockSpec((tm, d), lambda i: (i, 0)),
        pl.BlockSpec((tm, d), lambda i: (i, 0)),
        pl.BlockSpec((8, tm), lambda i: (0, i)),
        pl.BlockSpec((tm, LANES), lambda i: (i, 0)),
        pl.BlockSpec((8, tm), lambda i: (0, i)),
        pl.BlockSpec((8, LANES), lambda i: (0, 0)),
    )
    n_proj_views = 6 + 3 * gate_blocks_per_branch
    return pl.pallas_call(
        functools.partial(_mix_kernel, tm=tm, tiles_per_seq=tiles_per_seq),
        grid=(t // tm,),
        in_specs=in_specs,
        out_specs=out_specs,
        out_shape=out_shape,
        scratch_shapes=[pltpu.VMEM((1, LANES), F32)],
        compiler_params=pltpu.CompilerParams(
            dimension_semantics=("arbitrary",), vmem_limit_bytes=VMEM_LIMIT),
        name="mixer",
    )(x2, osb, *([proj] * n_proj_views), kmT, vmem, conv_w, wsb, wcv, wmm, wo, fg, wr_hi, wr_lo, br_pad, ltri)


DMA_UNROLL = 8


def _start_row_copies(n, make_copy):
    def body(g, c):
        for u in range(DMA_UNROLL):
            make_copy(g * DMA_UNROLL + u).start(priority=u % 2)
        return c

    lax.fori_loop(0, n // DMA_UNROLL, body, 0)


def _dispatch_kernel(dest_ref, pad_ref, hf_ref, xs_hbm, rows_scr, zrow, sems):
    i = pl.program_id(0)
    tm = hf_ref.shape[0]
    n_pad = pad_ref.shape[0]

    @pl.when(i == 0)
    def _():
        zrow[...] = jnp.zeros_like(zrow)
        _start_row_copies(n_pad, lambda j: pltpu.make_async_copy(zrow.at[0], xs_hbm.at[pad_ref[j]], sems.at[1]))

    rows_scr[...] = hf_ref[...].reshape(rows_scr.shape)
    n_tok = pl.num_programs(0) * tm
    first = i * tm

    def token_group(g, c):
        for u in range(DMA_UNROLL // TOP_K):
            r = g * (DMA_UNROLL // TOP_K) + u
            for k in range(TOP_K):
                pltpu.make_async_copy(rows_scr.at[r], xs_hbm.at[dest_ref[k * n_tok + first + r]],
                                      sems.at[0]).start(priority=k % 2)
        return c

    lax.fori_loop(0, tm * TOP_K // DMA_UNROLL, token_group, 0)
    for _ in range(TOP_K):
        pltpu.make_async_copy(rows_scr, xs_hbm.at[pl.ds(0, tm)], sems.at[0]).wait()

    @pl.when(i == pl.num_programs(0) - 1)
    def _():
        rows = xs_hbm.at[pl.ds(0, n_pad)]
        pltpu.make_async_copy(rows, rows, sems.at[1]).wait()


def _dispatch(dest_flat, pad_dest, hf, n_rows_total, tm):
    t, d = hf.shape
    grid_spec = pltpu.PrefetchScalarGridSpec(
        num_scalar_prefetch=2,
        grid=(t // tm,),
        in_specs=[pl.BlockSpec((tm, d), lambda i, dr, pr: (i, 0))],
        out_specs=pl.BlockSpec(memory_space=pl.ANY),
        scratch_shapes=[pltpu.VMEM((tm, 1, d), F32), pltpu.VMEM((8, 1, d), F32),
                        pltpu.SemaphoreType.DMA((2,))],
    )
    return pl.pallas_call(
        _dispatch_kernel,
        grid_spec=grid_spec,
        out_shape=jax.ShapeDtypeStruct((n_rows_total, 1, d), F32),
        compiler_params=pltpu.CompilerParams(
            dimension_semantics=("arbitrary",), vmem_limit_bytes=VMEM_LIMIT),
        name="dispatch",
    )(dest_flat, pad_dest, hf)


def _expert_kernel(be_ref, nu_ref, slot_ref, nxt_ref, x_ref, wgu_hbm, bgu_ref, wdn_hbm, bdn_ref, y_ref,
                   x2d, wgu_f32, wdn_f32, wgu_bf, wdn_bf, sems):
    i = pl.program_id(0)
    n_used = nu_ref[0]

    def weight_copies(e, slot):
        return (pltpu.make_async_copy(wgu_hbm.at[e], wgu_f32.at[slot], sems.at[0, slot]),
                pltpu.make_async_copy(wdn_hbm.at[e], wdn_f32.at[slot], sems.at[1, slot]))

    first_of_expert = jnp.logical_or(i == 0, be_ref[i] != be_ref[jnp.maximum(i - 1, 0)])
    for slot in range(2):
        @pl.when(jnp.logical_and(jnp.logical_and(first_of_expert, i < n_used), slot_ref[i] == slot))
        def _():
            @pl.when(i == 0)
            def _():
                for cp in weight_copies(be_ref[0], slot):
                    cp.start()

            for cp in weight_copies(be_ref[i], slot):
                cp.wait()
            wgu_bf[...] = wgu_f32[slot].astype(BF16)
            wdn_bf[...] = wdn_f32[slot].astype(BF16)

            @pl.when(nxt_ref[i] >= 0)
            def _():
                for cp in weight_copies(nxt_ref[i], 1 - slot):
                    cp.start()

    @pl.when(i < n_used)
    def _():
        x2d[...] = x_ref[...].reshape(x2d.shape)
        gu = _dot(x2d[...].astype(BF16), wgu_bf[...]) + bgu_ref[be_ref[i]]
        f = gu.shape[1] // 2
        g = jnp.minimum(gu[:, :f], SWIGLU_LIMIT)
        lin = jnp.clip(gu[:, f:], -SWIGLU_LIMIT, SWIGLU_LIMIT)
        act = g * (0.5 * jnp.tanh((0.5 * SWIGLU_ALPHA) * g) + 0.5) * (lin + 1.0)
        y = _dot(act.astype(BF16), wdn_bf[...]) + bdn_ref[be_ref[i]]
        y_ref[...] = y.reshape(y_ref.shape)

    @pl.when(i >= n_used)
    def _():
        y_ref[...] = jnp.zeros_like(y_ref)


def _experts(block_e, n_used, w_slot, next_e, xs3, n_rows, wgu, bgu, wdn, bdn, tmoe):
    d = xs3.shape[-1]
    e, _, f2 = wgu.shape

    def x_map(i, be, nu, ws, ne):
        return (jnp.minimum(i, jnp.maximum(nu[0] - 1, 0)), 0, 0)

    grid_spec = pltpu.PrefetchScalarGridSpec(
        num_scalar_prefetch=4,
        grid=(n_rows // tmoe,),
        in_specs=[
            pl.BlockSpec((tmoe, 1, d), x_map),
            pl.BlockSpec(memory_space=pl.ANY),
            pl.BlockSpec((e, 1, f2), lambda i, be, nu, ws, ne: (0, 0, 0)),
            pl.BlockSpec(memory_space=pl.ANY),
            pl.BlockSpec((e, 1, d), lambda i, be, nu, ws, ne: (0, 0, 0)),
        ],
        out_specs=pl.BlockSpec((tmoe, 1, d), lambda i, be, nu, ws, ne: (i, 0, 0)),
        scratch_shapes=[
            pltpu.VMEM((tmoe, d), F32),
            pltpu.VMEM((2, d, f2), F32),
            pltpu.VMEM((2, f2 // 2, d), F32),
            pltpu.VMEM((d, f2), BF16),
            pltpu.VMEM((f2 // 2, d), BF16),
            pltpu.SemaphoreType.DMA((2, 2)),
        ],
    )
    return pl.pallas_call(
        _expert_kernel,
        grid_spec=grid_spec,
        out_shape=jax.ShapeDtypeStruct((n_rows, 1, d), F32),
        compiler_params=pltpu.CompilerParams(
            dimension_semantics=("arbitrary",), vmem_limit_bytes=VMEM_LIMIT),
        name="experts",
    )(block_e, n_used, w_slot, next_e, xs3, wgu, bgu, wdn, bdn)


def _combine_kernel(dest_ref, x1_ref, gate_ref, y_hbm, o_ref, buf0, buf1, y2d, sems):
    i = pl.program_id(0)
    n = pl.num_programs(0)
    rows, _, d = buf0.shape
    tm = rows // TOP_K
    bufs = (buf0, buf1)

    def start_gather(tile, slot):
        first = tile * rows
        _start_row_copies(rows, lambda r: pltpu.make_async_copy(
            y_hbm.at[dest_ref[first + r]], bufs[slot].at[r], sems.at[slot]))

    @pl.when(i == 0)
    def _():
        start_gather(0, 0)

    for par in range(2):
        @pl.when(i % 2 == par)
        def _():
            @pl.when(i + 1 < n)
            def _():
                start_gather(i + 1, 1 - par)

            pltpu.make_async_copy(y_hbm.at[pl.ds(0, rows)], bufs[par], sems.at[par]).wait()
            y2d[...] = bufs[par][...].reshape(rows, d)

    acc = x1_ref[...]
    gate = gate_ref[...]
    for k in range(TOP_K):
        acc = acc + gate[:, k:k + 1] * y2d[k * tm:(k + 1) * tm, :]
    o_ref[...] = acc


def _combine(dest_tiled, x1, gate, y3, tm):
    t, d = x1.shape
    rows = tm * TOP_K
    grid_spec = pltpu.PrefetchScalarGridSpec(
        num_scalar_prefetch=1,
        grid=(t // tm,),
        in_specs=[
            pl.BlockSpec((tm, d), lambda i, dr: (i, 0)),
            pl.BlockSpec((tm, LANES), lambda i, dr: (i, 0)),
            pl.BlockSpec(memory_space=pl.ANY),
        ],
        out_specs=pl.BlockSpec((tm, d), lambda i, dr: (i, 0)),
        scratch_shapes=[
            pltpu.VMEM((rows, 1, d), F32),
            pltpu.VMEM((rows, 1, d), F32),
            pltpu.VMEM((rows, d), F32),
            pltpu.SemaphoreType.DMA((2,)),
        ],
    )
    return pl.pallas_call(
        _combine_kernel,
        grid_spec=grid_spec,
        out_shape=jax.ShapeDtypeStruct((t, d), F32),
        compiler_params=pltpu.CompilerParams(
            dimension_semantics=("arbitrary",), vmem_limit_bytes=VMEM_LIMIT),
        name="combine",
    )(dest_tiled, x1, gate, y3)


def _layer(x, mem, mix_norm_g, w_in, sb_q_norm_g, sb_k_norm_g, conv_w, mem_norm_g, w_mem_kv,
           mem_q_norm_g, mem_k_norm_g, w_br_sb, w_br_conv, w_br_mem, w_o, ffn_norm_g,
           w_router, b_router, w_gate_up, b_gate_up, w_down, b_down):
    b, s, d = x.shape
    m = mem.shape[1]
    t = b * s
    n_in = w_in.shape[1]
    sb_blk = 128
    sb_tq = 256
    tm_proj = 512
    tm_mix = 512
    tmoe = 256
    tm_comb = 512
    tm_disp = 1024

    x2 = x.reshape(t, d)
    nm64 = _group_sum_matrix(COL_BLOCK, SB_HEAD_DIM)
    nm128 = _group_sum_matrix(COL_BLOCK, MEM_HEAD_DIM)

    col_gain = jnp.ones((n_in,), F32)
    col_gain = col_gain.at[CB_Q * COL_BLOCK:(CB_Q + 1) * COL_BLOCK].set(
        jnp.tile(sb_q_norm_g, SB_HEADS) * (SB_HEAD_DIM ** -0.5 * LOG2_E))
    col_gain = col_gain.at[CB_K * COL_BLOCK:(CB_K + 1) * COL_BLOCK].set(jnp.tile(sb_k_norm_g, SB_HEADS))
    col_gain = col_gain.at[CB_MEM_Q * COL_BLOCK:(CB_MEM_Q + 1) * COL_BLOCK].set(
        jnp.tile(mem_q_norm_g, MEM_HEADS) * (MEM_HEAD_DIM ** -0.5))

    proj, kT = _inproj(x2, mix_norm_g.reshape(1, d), w_in.astype(BF16),
                       jnp.stack([nm64, nm128]), col_gain.reshape(1, n_in), tm_proj)

    km, vm = _memkv(mem.reshape(b * m, d), mem_norm_g.reshape(1, d), w_mem_kv.astype(BF16), nm128,
                    jnp.tile(mem_k_norm_g, MEM_HEADS).reshape(1, -1))
    kmT = km.reshape(b, m, -1).transpose(0, 2, 1)

    jj = np.arange(sb_blk)
    later = (jj[:, None] >= jj[None, :]).astype(np.float32)
    ones = np.ones((sb_blk, sb_blk), np.float32)
    eye = np.eye(2, dtype=np.float32)
    tri = np.kron(eye, later)
    z_bound = (SB_HEAD_DIM * (SB_HEAD_DIM ** -0.5 * LOG2_E) * BF16_SLACK) * jnp.max(
        jnp.abs(sb_q_norm_g * sb_k_norm_g))
    stop_at = (F32_UNDERFLOW_LOG2 + z_bound).reshape(1).astype(F32)
    o_sb = _sb_attention(stop_at, proj, kT, jnp.asarray(tri, dtype=BF16), b, s, sb_blk, sb_tq)

    wr_pad = jnp.zeros((d, LANES), F32).at[:, :N_EXPERTS].set(w_router)
    wr_hi, wr_lo = _split_bf16(wr_pad)
    br_pad = jnp.full((1, LANES), PAD_LOGIT, F32).at[0, :N_EXPERTS].set(b_router)
    rr = np.arange(tm_mix)
    ltri = jnp.asarray(rr[None, :] < rr[:, None], dtype=BF16)
    x1, hf, idx, gate, rank, cnt = _mixer(
        x2, o_sb, proj, kmT, vm, conv_w, w_br_sb.astype(BF16), w_br_conv.astype(BF16),
        w_br_mem.astype(BF16), w_o.astype(BF16), ffn_norm_g.reshape(1, d), wr_hi, wr_lo, br_pad, ltri,
        s, m, tm_mix)

    a = t * TOP_K
    counts = cnt[0, :N_EXPERTS].astype(jnp.int32)
    padded = (counts + tmoe - 1) // tmoe * tmoe
    pend = jnp.cumsum(padded)
    pstart = pend - padded
    experts_col = jnp.arange(N_EXPERTS, dtype=jnp.int32)[:, None, None]
    dest = rank[:TOP_K] + jnp.sum(jnp.where(idx[None, :TOP_K] == experts_col, pstart[:, None, None], 0), axis=0)
    n_blocks = a // tmoe + N_EXPERTS
    n_rows = n_blocks * tmoe
    block_row0 = jnp.arange(n_blocks, dtype=jnp.int32) * tmoe
    block_e = jnp.minimum(jnp.sum((pend[None, :] <= block_row0[:, None]).astype(jnp.int32), axis=1),
                          N_EXPERTS - 1)
    n_used = (pend[-1] // tmoe).astype(jnp.int32).reshape(1)
    slot = jnp.arange(tmoe, dtype=jnp.int32)
    n_pad = padded - counts
    is_pad = slot[None, :] < n_pad[:, None]
    spare_index = (jnp.arange(N_EXPERTS, dtype=jnp.int32) * tmoe - jnp.cumsum(n_pad))[:, None] + slot[None, :]
    pad_dest = jnp.where(is_pad, (pstart + counts)[:, None] + slot[None, :], pend[-1] + spare_index).reshape(-1)
    xs3 = _dispatch(dest.reshape(a), pad_dest, hf, n_rows, tm_disp)
    has_rows = counts > 0
    e_ids = jnp.arange(N_EXPERTS, dtype=jnp.int32)
    is_block_expert = block_e[:, None] == e_ids[None, :]

    def per_block(per_expert):
        return jnp.sum(jnp.where(is_block_expert, per_expert[None, :], 0), axis=1)

    w_slot = per_block((jnp.cumsum(has_rows.astype(jnp.int32)) - 1) % 2)
    later_e = jnp.where(has_rows[None, :] & (e_ids[None, :] > e_ids[:, None]), e_ids[None, :], N_EXPERTS)
    next_e = jnp.min(later_e, axis=1)
    next_e = per_block(jnp.where(next_e < N_EXPERTS, next_e, -1))
    y3 = _experts(block_e, n_used, w_slot.astype(jnp.int32), next_e.astype(jnp.int32), xs3, n_rows,
                  w_gate_up, b_gate_up[:, None, :], w_down, b_down[:, None, :], tmoe)
    dest_tiled = dest.reshape(TOP_K, t // tm_comb, tm_comb).transpose(1, 0, 2).reshape(a)
    out = _combine(dest_tiled, x1, gate, y3, tm_comb)
    return out.reshape(b, s, d)


def kernel(x, mem, mix_norm_g, w_in, sb_q_norm_g, sb_k_norm_g, conv_w, mem_norm_g, w_mem_kv,
           mem_q_norm_g, mem_k_norm_g, w_br_sb, w_br_conv, w_br_mem, w_o, ffn_norm_g,
           w_router, b_router, w_gate_up, b_gate_up, w_down, b_down):
    depth = mix_norm_g.shape[0]
    for l in range(depth):
        x = _layer(x, mem, mix_norm_g[l], w_in[l], sb_q_norm_g[l], sb_k_norm_g[l], conv_w[l],
                   mem_norm_g[l], w_mem_kv[l], mem_q_norm_g[l], mem_k_norm_g[l], w_br_sb[l],
                   w_br_conv[l], w_br_mem[l], w_o[l], ffn_norm_g[l], w_router[l], b_router[l],
                   w_gate_up[l], b_gate_up[l], w_down[l], b_down[l])
    return x
```
